```python
import jax, jax.numpy as jnp
from jax import lax
import numpy as np

D_MODEL = 2048
BATCH = 16
SEQ = 2048
DEPTH = 1
DEC_BATCH = 32
DEC_SEQ = 32
PAST_LEN = 1024

CHUNK = 64
POOL_WINDOWS = (2, 4, 8, 16)
N_POOL_GROUPS = 4
POOL_WIDTH = D_MODEL // 2
POOL_GROUP = POOL_WIDTH // N_POOL_GROUPS
POOL_STATE = max(POOL_WINDOWS) - 1
MLSTM_HEADS = 8
MLSTM_WIDTH = D_MODEL
MLSTM_HEAD_DIM = MLSTM_WIDTH // MLSTM_HEADS
D_FF = -(-8 * D_MODEL // (3 * 256)) * 256
ALPHA = (2 * DEPTH) ** 0.25
BETA = (8 * DEPTH) ** -0.25
LN_EPS = 1e-5
SPLIT_IDX = [POOL_WIDTH,
             POOL_WIDTH + MLSTM_WIDTH,
             POOL_WIDTH + 2 * MLSTM_WIDTH,
             POOL_WIDTH + 3 * MLSTM_WIDTH,
             POOL_WIDTH + 4 * MLSTM_WIDTH,
             POOL_WIDTH + 4 * MLSTM_WIDTH + MLSTM_HEADS,
             POOL_WIDTH + 4 * MLSTM_WIDTH + 2 * MLSTM_HEADS,
             POOL_WIDTH + 4 * MLSTM_WIDTH + 2 * MLSTM_HEADS + D_MODEL]
N_IN = POOL_WIDTH + 4 * MLSTM_WIDTH + 2 * MLSTM_HEADS + 2 * D_MODEL

kernel_name = "pool_mlstm_gated_deepnorm_adaln_stream_step"


def layer_norm(x, g=None, b=None):
    xf = x.astype(jnp.float32)
    mu = jnp.mean(xf, axis=-1, keepdims=True)
    var = jnp.mean(jnp.square(xf - mu), axis=-1, keepdims=True)
    y = (xf - mu) * lax.rsqrt(var + LN_EPS)
    if g is not None:
        y = y * g.astype(jnp.float32) + b.astype(jnp.float32)
    return y.astype(x.dtype)


def pool_mixer(p, prefix, start_pos, w_pool, pool_scale):
    B, L, _ = p.shape
    ext = jnp.concatenate([prefix.astype(p.dtype), p], axis=1)
    ef = ext.astype(jnp.float32)
    cs = jnp.concatenate([jnp.zeros_like(ef[:, :1]), jnp.cumsum(ef, axis=1)], axis=1)
    pos = start_pos + jnp.arange(L)
    hi = cs[:, POOL_STATE + 1:POOL_STATE + 1 + L]
    tok = ef[:, POOL_STATE:]
    outs = []
    for g, w in enumerate(POOL_WINDOWS):
        sl = slice(g * POOL_GROUP, (g + 1) * POOL_GROUP)
        lo = cs[:, POOL_STATE + 1 - w:POOL_STATE + 1 - w + L, sl]
        cnt = jnp.minimum(pos + 1, w).astype(jnp.float32)[None, :, None]
        outs.append((hi[..., sl] - lo) / cnt - tok[..., sl])
    y = jnp.stack(outs, axis=2).astype(p.dtype)
    y = jnp.einsum('blgc,gcd->blgd', y, w_pool).reshape(B, L, POOL_WIDTH) * pool_scale
    return y, ext[:, -POOL_STATE:]


def mlstm_chunk(carry, inp):
    C, n, m = carry
    q, k, v, ig, lf = inp
    L = q.shape[2]
    b = jnp.cumsum(lf, axis=-1)
    a = b + m[..., None]
    causal = jnp.tril(jnp.ones((L, L), dtype=bool))
    log_d = jnp.where(causal, b[..., :, None] - b[..., None, :] + ig[..., None, :], -jnp.inf)
    m_row = jnp.maximum(a, jnp.max(log_d, axis=-1))
    d = jnp.exp(log_d - m_row[..., None])
    w_inter = jnp.exp(a - m_row)
    s = jnp.einsum('bhld,bhsd->bhls', q, k) * d
    num = w_inter[..., None] * jnp.einsum('bhld,bhde->bhle', q, C) + jnp.einsum('bhls,bhse->bhle', s, v)
    den = w_inter * jnp.einsum('bhld,bhd->bhl', q, n) + jnp.sum(s, axis=-1)
    h = num / jnp.maximum(jnp.abs(den), jnp.exp(-m_row))[..., None]
    m_new = m_row[..., -1]
    w_s = jnp.exp(b[..., -1:] - b + ig - m_new[..., None])
    decay = jnp.exp(b[..., -1] + m - m_new)
    kw = k * w_s[..., None]
    C_new = decay[..., None, None] * C + jnp.einsum('bhsd,bhse->bhde', kw, v)
    n_new = decay[..., None] * n + jnp.sum(kw, axis=2)
    return (C_new, n_new, m_new), h


def mlstm_seq(q, k, v, ig, lf, C, n, m):
    L = q.shape[2]
    if L <= CHUNK:
        (C, n, m), h = mlstm_chunk((C, n, m), (q, k, v, ig, lf))
        return h, C, n, m
    nc = L // CHUNK
    def split(t):
        return jnp.moveaxis(t.reshape(t.shape[:2] + (nc, CHUNK) + t.shape[3:]), 2, 0)
    (C, n, m), h = lax.scan(mlstm_chunk, (C, n, m), (split(q), split(k), split(v), split(ig), split(lf)))
    h = jnp.moveaxis(h, 0, 2).reshape(q.shape[:2] + (L, q.shape[-1]))
    return h, C, n, m


def token_mixer(u, pool_prefix, C0, n0, m0, start_pos, w_in, b_i, b_f, w_pool, pool_scale, gn_w, w_pa, w_pb, w_out):
    B, L, _ = u.shape
    f32 = jnp.float32
    proj = u @ w_in
    p, q, k, v, o, ig, fg, ga, gb = jnp.split(proj, SPLIT_IDX, axis=-1)
    a_out, pool_state = pool_mixer(p, pool_prefix, start_pos, w_pool, pool_scale)
    def heads(t):
        return jnp.swapaxes(t.reshape(B, L, MLSTM_HEADS, MLSTM_HEAD_DIM).astype(f32), 1, 2)
    ig_t = jnp.swapaxes((ig + b_i).astype(f32), 1, 2)
    lf_t = jax.nn.log_sigmoid(jnp.swapaxes((fg + b_f).astype(f32), 1, 2))
    h, C, n, m = mlstm_seq(heads(q), heads(k) * (MLSTM_HEAD_DIM ** -0.5), heads(v), ig_t, lf_t,
                           C0.astype(f32), n0.astype(f32), m0.astype(f32))
    h = jnp.swapaxes(h, 1, 2)
    h = h * jax.nn.sigmoid(o.astype(f32)).reshape(B, L, MLSTM_HEADS, MLSTM_HEAD_DIM)
    mu = jnp.mean(h, axis=-1, keepdims=True)
    var = jnp.mean(jnp.square(h - mu), axis=-1, keepdims=True)
    h = (h - mu) * lax.rsqrt(var + LN_EPS) * gn_w.astype(f32).reshape(MLSTM_HEADS, MLSTM_HEAD_DIM)
    b_out = h.reshape(B, L, MLSTM_WIDTH).astype(u.dtype)
    merged = jax.nn.sigmoid(ga) * (a_out @ w_pa) + jax.nn.sigmoid(gb) * (b_out @ w_pb)
    return merged @ w_out, pool_state, C.astype(u.dtype), n.astype(u.dtype), m.astype(u.dtype)


def layer(x, c, pool_prefix, C0, n0, m0, start_pos, w_ada, b_ada, w_in, b_i, b_f, w_pool, pool_scale, gn_w,
          w_pa, w_pb, w_out, ln1_g, ln1_b, w_gate, w_up, w_down, ln2_g, ln2_b):
    mod = jax.nn.silu(c) @ w_ada + b_ada
    sh1, sc1, g1, sh2, sc2, g2 = [t[:, None, :] for t in jnp.split(mod, 6, axis=-1)]
    u = layer_norm(x) * (1 + sc1) + sh1
    t, pool_state, C, n, m = token_mixer(u, pool_prefix, C0, n0, m0, start_pos, w_in, b_i, b_f, w_pool,
                                         pool_scale, gn_w, w_pa, w_pb, w_out)
    x = layer_norm(ALPHA * x + g1 * t, ln1_g, ln1_b)
    u = layer_norm(x) * (1 + sc2) + sh2
    f = (jax.nn.silu(u @ w_gate) * (u @ w_up)) @ w_down
    x = layer_norm(ALPHA * x + g2 * f, ln2_g, ln2_b)
    return x, pool_state, C, n, m


def setup_inputs(seed: int = 0) -> dict:
    key = jax.random.key(seed)
    ks = jax.random.split(key, 32)
    f32 = jnp.float32
    def nrm(k, shape, s):
        return jax.random.normal(k, shape, f32) * s
    return {
        "x_prompt": nrm(ks[0], (BATCH, SEQ, D_MODEL), 1.0),
        "x_sample": nrm(ks[1], (DEC_BATCH, DEC_SEQ, D_MODEL), 1.0),
        "c_prompt": nrm(ks[2], (BATCH, D_MODEL), 1.0),
        "c_sample": nrm(ks[3], (DEC_BATCH, D_MODEL), 1.0),
        "state_pool": nrm(ks[4], (DEPTH, DEC_BATCH, POOL_STATE, POOL_WIDTH), 1.0),
        "state_mlstm_C": nrm(ks[5], (DEPTH, DEC_BATCH, MLSTM_HEADS, MLSTM_HEAD_DIM, MLSTM_HEAD_DIM), 0.05),
        "state_mlstm_n": nrm(ks[6], (DEPTH, DEC_BATCH, MLSTM_HEADS, MLSTM_HEAD_DIM), 0.05),
        "state_mlstm_m": nrm(ks[7], (DEPTH, DEC_BATCH, MLSTM_HEADS), 0.5),
        "w_ada": nrm(ks[8], (DEPTH, D_MODEL, 6 * D_MODEL), 0.5 * D_MODEL ** -0.5),
        "b_ada": nrm(ks[9], (DEPTH, 6 * D_MODEL), 0.02),
        "w_in": nrm(ks[10], (DEPTH, D_MODEL, N_IN), D_MODEL ** -0.5),
        "b_i": nrm(ks[11], (DEPTH, MLSTM_HEADS), 0.1),
        "b_f": jnp.linspace(3.0, 6.0, MLSTM_HEADS, dtype=f32)[None, :] + nrm(ks[12], (DEPTH, MLSTM_HEADS), 0.01),
        "w_pool": nrm(ks[13], (DEPTH, N_POOL_GROUPS, POOL_GROUP, POOL_GROUP), POOL_GROUP ** -0.5),
        "pool_scale": 1.0 + nrm(ks[14], (DEPTH, POOL_WIDTH), 0.02),
        "gn_w": 1.0 + nrm(ks[15], (DEPTH, MLSTM_WIDTH), 0.02),
        "w_pa": nrm(ks[16], (DEPTH, POOL_WIDTH, D_MODEL), POOL_WIDTH ** -0.5),
        "w_pb": nrm(ks[17], (DEPTH, MLSTM_WIDTH, D_MODEL), MLSTM_WIDTH ** -0.5),
        "w_out": nrm(ks[18], (DEPTH, D_MODEL, D_MODEL), BETA * D_MODEL ** -0.5),
        "ln1_g": 1.0 + nrm(ks[19], (DEPTH, D_MODEL), 0.02),
        "ln1_b": nrm(ks[20], (DEPTH, D_MODEL), 0.02),
        "w_gate": nrm(ks[21], (DEPTH, D_MODEL, D_FF), D_MODEL ** -0.5),
        "w_up": nrm(ks[22], (DEPTH, D_MODEL, D_FF), D_MODEL ** -0.5),
        "w_down": nrm(ks[23], (DEPTH, D_FF, D_MODEL), BETA * D_FF ** -0.5),
        "ln2_g": 1.0 + nrm(ks[24], (DEPTH, D_MODEL), 0.02),
        "ln2_b": nrm(ks[25], (DEPTH, D_MODEL), 0.02),
    }


def reference(x_prompt, x_sample, c_prompt, c_sample, state_pool, state_mlstm_C, state_mlstm_n, state_mlstm_m,
              w_ada, b_ada, w_in, b_i, b_f, w_pool, pool_scale, gn_w, w_pa, w_pb, w_out, ln1_g, ln1_b,
              w_gate, w_up, w_down, ln2_g, ln2_b):
    B = x_prompt.shape[0]
    y_prompt, y_sample = x_prompt, x_sample
    pool_p_l, C_p_l, n_p_l, m_p_l = [], [], [], []
    pool_s_l, C_s_l, n_s_l, m_s_l = [], [], [], []
    for l in range(DEPTH):
        wl = (w_ada[l], b_ada[l], w_in[l], b_i[l], b_f[l], w_pool[l], pool_scale[l], gn_w[l], w_pa[l], w_pb[l],
              w_out[l], ln1_g[l], ln1_b[l], w_gate[l], w_up[l], w_down[l], ln2_g[l], ln2_b[l])
        prefix0 = jnp.zeros((B, POOL_STATE, POOL_WIDTH), x_prompt.dtype)
        C0 = jnp.zeros((B, MLSTM_HEADS, MLSTM_HEAD_DIM, MLSTM_HEAD_DIM), jnp.float32)
        n0 = jnp.zeros((B, MLSTM_HEADS, MLSTM_HEAD_DIM), jnp.float32)
        m0 = jnp.zeros((B, MLSTM_HEADS), jnp.float32)
        y_prompt, ps, Cp, npp, mp = layer(y_prompt, c_prompt, prefix0, C0, n0, m0, 0, *wl)
        y_sample, ss, Cs, ns, ms = layer(y_sample, c_sample, state_pool[l], state_mlstm_C[l], state_mlstm_n[l],
                                         state_mlstm_m[l], PAST_LEN, *wl)
        pool_p_l.append(ps); C_p_l.append(Cp); n_p_l.append(npp); m_p_l.append(mp)
        pool_s_l.append(ss); C_s_l.append(Cs); n_s_l.append(ns); m_s_l.append(ms)
    pool_p = jnp.stack(pool_p_l, axis=0)
    C_p = jnp.stack(C_p_l, axis=0)
    n_p = jnp.stack(n_p_l, axis=0)
    m_p = jnp.stack(m_p_l, axis=0)
    pool_s = jnp.stack(pool_s_l, axis=0)
    C_s = jnp.stack(C_s_l, axis=0)
    n_s = jnp.stack(n_s_l, axis=0)
    m_s = jnp.stack(m_s_l, axis=0)
    return (y_prompt, y_sample, pool_p, C_p, n_p, m_p, pool_s, C_s, n_s, m_s)
```

```python
import functools

import jax
import jax.numpy as jnp
from jax import lax
from jax.experimental import pallas as pl
from jax.experimental.pallas import tpu as pltpu

F32 = jnp.float32
BF16 = jnp.bfloat16

LN_EPS = 1e-5
POOL_WINDOWS = (2, 4, 8, 16)
POOL_STATE = max(POOL_WINDOWS) - 1
POOL_HALO = 16
PAST_LEN = 1024
LANES = 128
VMEM_CAP_BYTES = 60 * 1024 * 1024


def _vmem_limit(nbytes):
    return int(min(VMEM_CAP_BYTES, max(32 * 1024 * 1024, nbytes * 5 // 4)))


def _params(nbytes, ndims):
    return pltpu.CompilerParams(dimension_semantics=("arbitrary",) * ndims,
                                vmem_limit_bytes=_vmem_limit(nbytes))


def _layer_norm(x):
    mu = jnp.mean(x, axis=-1, keepdims=True)
    xc = x - mu
    var = jnp.mean(xc * xc, axis=-1, keepdims=True)
    return xc * lax.rsqrt(var + LN_EPS)


def _dot(a, b):
    return jnp.dot(a, b, preferred_element_type=F32)


def _log_sigmoid(x):
    return jnp.minimum(x, 0.0) - jnp.log1p(jnp.exp(-jnp.abs(x)))


def _split3(x):
    hi = x.astype(BF16)
    r1 = x - hi.astype(F32)
    mid = r1.astype(BF16)
    lo = (r1 - mid.astype(F32)).astype(BF16)
    return hi, mid, lo


def _ada_kernel(c_ref, w_ref, b_ref, o_ref):
    c = c_ref[...]
    a = (c * jax.nn.sigmoid(c)).astype(BF16)
    o_ref[...] = _dot(a, w_ref[...].astype(BF16)) + b_ref[...]


def _ada(c, w_ada, b_ada):
    nb, d = c.shape
    n = w_ada.shape[1]
    tn = 1024 if n % 1024 == 0 else n
    nbytes = 2 * (d * tn * 4) + d * tn * 2 + 4 * nb * (d + tn) * 4
    return pl.pallas_call(
        _ada_kernel,
        grid=(n // tn,),
        in_specs=[pl.BlockSpec((nb, d), lambda j: (0, 0)),
                  pl.BlockSpec((d, tn), lambda j: (0, j)),
                  pl.BlockSpec((1, tn), lambda j: (0, j))],
        out_specs=pl.BlockSpec((nb, tn), lambda j: (0, j)),
        out_shape=jax.ShapeDtypeStruct((nb, n), F32),
        compiler_params=_params(nbytes, 1),
        name="ada",
    )(c, w_ada, b_ada.reshape(1, n))


def _inproj_kernel(x_ref, sh_ref, sc_ref, wp_ref, wif_ref, u_ref, p_ref, ifg_ref):
    nb, lt, d = x_ref.shape
    u = _layer_norm(x_ref[...]) * (1.0 + sc_ref[...]) + sh_ref[...]
    ub = u.reshape(nb * lt, d).astype(BF16)
    u_ref[...] = ub
    p_ref[...] = _dot(ub, wp_ref[...])
    ifg_ref[...] = _dot(ub, wif_ref[...])


def _inproj(x, mod4, w_p, w_if, nb, lt):
    b, l, d = x.shape
    pw = w_p.shape[1]
    tm = nb * lt
    nt = l // lt
    rows = lambda bi, ti: (bi * nt + ti, 0)
    nbytes = (2 * tm * d * 4 + 2 * tm * d * 2 + 2 * tm * pw * 4 + 2 * d * pw * 2
              + 3 * tm * d * 4 + tm * pw * 4)
    return pl.pallas_call(
        _inproj_kernel,
        grid=(b // nb, nt),
        in_specs=[pl.BlockSpec((nb, lt, d), lambda bi, ti: (bi, ti, 0)),
                  pl.BlockSpec((nb, None, 1, d), lambda bi, ti: (bi, 0, 0, 0)),
                  pl.BlockSpec((nb, None, 1, d), lambda bi, ti: (bi, 1, 0, 0)),
                  pl.BlockSpec((d, pw), lambda bi, ti: (0, 0)),
                  pl.BlockSpec((d, LANES), lambda bi, ti: (0, 0))],
        out_specs=[pl.BlockSpec((tm, d), rows),
                   pl.BlockSpec((tm, pw), rows),
                   pl.BlockSpec((tm, LANES), rows)],
        out_shape=[jax.ShapeDtypeStruct((b * l, d), BF16),
                   jax.ShapeDtypeStruct((b * l, pw), F32),
                   jax.ShapeDtypeStruct((b * l, LANES), F32)],
        compiler_params=_params(nbytes, 2),
        name="inproj",
    )(x, mod4, mod4, w_p, w_if)


def _matmul_kernel(a_ref, b_ref, o_ref):
    o_ref[...] = _dot(a_ref[...], b_ref[...]).astype(o_ref.dtype)


def _matmul(a, b, tm, tn):
    m, k = a.shape
    n = b.shape[1]
    nbytes = 2 * tm * k * 2 + 2 * k * tn * 2 + 2 * tm * tn * 2 + tm * tn * 4
    return pl.pallas_call(
        _matmul_kernel,
        grid=(m // tm, n // tn),
        in_specs=[pl.BlockSpec((tm, k), lambda i, j: (i, 0)),
                  pl.BlockSpec((k, tn), lambda i, j: (0, j))],
        out_specs=pl.BlockSpec((tm, tn), lambda i, j: (i, j)),
        out_shape=jax.ShapeDtypeStruct((m, n), BF16),
        compiler_params=_params(nbytes, 2),
        name="bigproj",
    )(a, b)


def _mlstm_kernel(*refs, nh, hd, zero_init):
    if zero_init:
        (q_ref, k_ref, v_ref, o_ref, ifg_ref, bias_ref, gnw_ref,
         h_ref, c_out, n_out, m_out, c_scr, n_scr, m_scr) = refs
    else:
        (q_ref, k_ref, v_ref, o_ref, ifg_ref, bias_ref, gnw_ref, c0_ref, n0_ref, m0_ref,
         h_ref, c_out, n_out, m_out, c_scr, n_scr, m_scr) = refs
    lc = q_ref.shape[0]
    ci = pl.program_id(1)

    @pl.when(ci == 0)
    def _():
        if zero_init:
            c_scr[...] = jnp.zeros_like(c_scr)
            n_scr[...] = jnp.zeros_like(n_scr)
            m_scr[...] = jnp.zeros_like(m_scr)
        else:
            c_scr[...] = c0_ref[0]
            n_scr[...] = n0_ref[0]
            m_scr[...] = m0_ref[0]

    gates = ifg_ref[...] + bias_ref[...]
    lf = _log_sigmoid(gates)
    row_t = lax.broadcasted_iota(jnp.int32, (lc, lc), 0)
    col_s = lax.broadcasted_iota(jnp.int32, (lc, lc), 1)
    causal = col_s <= row_t
    tril = jnp.where(causal, 1.0, 0.0).astype(BF16)
    b_all = sum(_dot(tril, piece) for piece in _split3(lf))
    b_col = pltpu.roll(b_all, LANES - nh, 1)
    g_col = gates - b_col
    g_row = jnp.transpose(g_col)
    scale = hd ** -0.5

    for h in range(nh):
        sl = slice(h * hd, (h + 1) * hd)
        q = q_ref[:, sl]
        k = (k_ref[:, sl].astype(F32) * scale).astype(BF16)
        v = v_ref[:, sl]
        m_prev = m_scr[h]
        n_prev = n_scr[h]
        c_prev = c_scr[h]

        x = jnp.where(causal, g_row[h:h + 1, :], -jnp.inf)
        m_run = jnp.maximum(jnp.max(x, axis=1, keepdims=True), m_prev)
        dmat = jnp.exp(x - m_run)
        w_inter = jnp.exp(m_prev - m_run)

        qk = lax.dot_general(q, k, (((1,), (1,)), ((), ())), preferred_element_type=F32)
        s = qk * dmat
        num = w_inter * _dot(q, c_prev.astype(BF16)) + _dot(s.astype(BF16), v)
        den = (w_inter * jnp.sum(q.astype(F32) * n_prev, axis=1, keepdims=True)
               + jnp.sum(s, axis=1, keepdims=True))
        m_row = b_col[:, h:h + 1] + m_run
        hh = num / jnp.maximum(jnp.abs(den), jnp.exp(-m_row))

        hg = hh * jax.nn.sigmoid(o_ref[:, sl].astype(F32))
        h_ref[:, sl] = (_layer_norm(hg) * gnw_ref[:, sl]).astype(BF16)

        m_last = m_run[lc - 1:lc, :]
        w_s = jnp.exp(g_col[:, h:h + 1] - m_last)
        kw = k.astype(F32) * w_s
        decay = jnp.exp(m_prev - m_last)
        c_scr[h] = decay * c_prev + lax.dot_general(
            kw.astype(BF16), v, (((0,), (0,)), ((), ())), preferred_element_type=F32)
        n_scr[h] = decay * n_prev + jnp.sum(kw, axis=0, keepdims=True)
        m_scr[h] = b_col[lc - 1:lc, h:h + 1] + m_last

    @pl.when(ci == pl.num_programs(1) - 1)
    def _():
        c_out[0] = c_scr[...]
        n_out[0] = n_scr[...]
        m_out[0] = m_scr[...]


def _mlstm(big, ifg, gate_bias, gn_w, state, b, l, lc, nh):
    mw = gn_w.shape[1]
    hd = mw // nh
    nc = l // lc
    rows = lambda bi, ci: (bi * nc + ci, 0)
    col = lambda j: (lambda bi, ci: (bi * nc + ci, j))
    st4 = lambda bi, ci: (bi, 0, 0, 0)
    zero_init = state is None
    in_specs = [pl.BlockSpec((lc, mw), col(0)), pl.BlockSpec((lc, mw), col(1)),
                pl.BlockSpec((lc, mw), col(2)), pl.BlockSpec((lc, mw), col(3)),
                pl.BlockSpec((lc, LANES), rows),
                pl.BlockSpec((1, LANES), lambda bi, ci: (0, 0)),
                pl.BlockSpec((1, mw), lambda bi, ci: (0, 0))]
    args = [big, big, big, big, ifg, gate_bias, gn_w]
    if not zero_init:
        c0, n0, m0 = state
        in_specs += [pl.BlockSpec((1, nh, hd, hd), st4), pl.BlockSpec((1, nh, 1, hd), st4),
                     pl.BlockSpec((1, nh, 1, 1), st4)]
        args += [c0, n0.reshape(b, nh, 1, hd), m0.reshape(b, nh, 1, 1)]
    nbytes = (2 * 5 * lc * mw * 2 + 4 * nh * hd * hd * 4 + nh * hd * hd * 4
              + 16 * lc * max(lc, hd) * 4)
    h, c, n, m = pl.pallas_call(
        functools.partial(_mlstm_kernel, nh=nh, hd=hd, zero_init=zero_init),
        grid=(b, nc),
        in_specs=in_specs,
        out_specs=[pl.BlockSpec((lc, mw), rows), pl.BlockSpec((1, nh, hd, hd), st4),
                   pl.BlockSpec((1, nh, 1, hd), st4), pl.BlockSpec((1, nh, 1, 1), st4)],
        out_shape=[jax.ShapeDtypeStruct((b * l, mw), BF16),
                   jax.ShapeDtypeStruct((b, nh, hd, hd), F32),
                   jax.ShapeDtypeStruct((b, nh, 1, hd), F32),
                   jax.ShapeDtypeStruct((b, nh, 1, 1), F32)],
        scratch_shapes=[pltpu.VMEM((nh, hd, hd), F32), pltpu.VMEM((nh, 1, hd), F32),
                        pltpu.VMEM((nh, 1, 1), F32)],
        compiler_params=_params(nbytes, 2),
        name="mlstm",
    )(*args)
    return h, c, n.reshape(b, nh, hd), m.reshape(b, nh)


def _mix_kernel(*refs, start_pos, alpha, has_prefix):
    if has_prefix:
        (p_ref, pre_ref, bo_ref, ga_ref, gb_ref, x_ref, g1_ref, sh2_ref, sc2_ref, wpool_ref, psc_ref,
         wpa_ref, wpb_ref, wout_ref, l1g_ref, l1b_ref, x1_ref, u2_ref, ext_scr) = refs
    else:
        (p_ref, bo_ref, ga_ref, gb_ref, x_ref, g1_ref, sh2_ref, sc2_ref, wpool_ref, psc_ref,
         wpa_ref, wpb_ref, wout_ref, l1g_ref, l1b_ref, x1_ref, u2_ref, ext_scr) = refs
    nb, lt, pw = p_ref.shape
    d = x_ref.shape[2]
    grp = pw // len(POOL_WINDOWS)
    ti = pl.program_id(1)

    @pl.when(ti == 0)
    def _():
        ext_scr[:, 0:POOL_HALO, :] = jnp.zeros((nb, POOL_HALO, pw), F32)
        if has_prefix:
            ext_scr[:, POOL_HALO - POOL_STATE:POOL_HALO, :] = pre_ref[...]

    @pl.when(ti > 0)
    def _():
        ext_scr[:, 0:POOL_HALO, :] = ext_scr[:, lt:lt + POOL_HALO, :]

    ext_scr[:, POOL_HALO:POOL_HALO + lt, :] = p_ref[...]

    pos = start_pos + ti * lt + lax.broadcasted_iota(jnp.int32, (1, lt, grp), 1)
    branch = []
    for g, w in enumerate(POOL_WINDOWS):
        cs = slice(g * grp, (g + 1) * grp)
        tok = ext_scr[:, POOL_HALO:POOL_HALO + lt, cs]
        acc = tok
        for kk in range(1, w):
            acc = acc + ext_scr[:, POOL_HALO - kk:POOL_HALO - kk + lt, cs]
        cnt = jnp.minimum(pos + 1, w).astype(F32)
        y = (acc / cnt - tok).reshape(nb * lt, grp).astype(BF16)
        branch.append((_dot(y, wpool_ref[g]) * psc_ref[:, cs]).astype(BF16))
    a_out = jnp.concatenate(branch, axis=1)

    pa = _dot(a_out, wpa_ref[...])
    pb = _dot(bo_ref[...], wpb_ref[...])
    merged = (jax.nn.sigmoid(ga_ref[...].astype(F32)) * pa
              + jax.nn.sigmoid(gb_ref[...].astype(F32)) * pb).astype(BF16)
    t = _dot(merged, wout_ref[...]).reshape(nb, lt, d)
    x1 = _layer_norm(alpha * x_ref[...] + g1_ref[...] * t) * l1g_ref[...] + l1b_ref[...]
    x1_ref[...] = x1
    u2 = _layer_norm(x1) * (1.0 + sc2_ref[...]) + sh2_ref[...]
    u2_ref[...] = u2.reshape(nb * lt, d).astype(BF16)


def _mix(p, prefix, b_out, big, x, mod4, w_pool, pool_scale, w_pa, w_pb, w_out, ln1_g, ln1_b,
         nb, lt, start_pos, alpha):
    b, l, d = x.shape
    pw = p.shape[1]
    mw = b_out.shape[1]
    grp = pw // len(POOL_WINDOWS)
    tm = nb * lt
    nt = l // lt
    rows = lambda bi, ti: (bi * nt + ti, 0)
    seq3 = lambda bi, ti: (bi, ti, 0)
    const2 = lambda bi, ti: (0, 0)
    modk = lambda kk: (lambda bi, ti: (bi, kk, 0, 0))
    has_prefix = prefix is not None
    single = pl.Buffered(1)
    in_specs = [pl.BlockSpec((nb, lt, pw), seq3)]
    args = [p.reshape(b, l, pw)]
    if has_prefix:
        in_specs.append(pl.BlockSpec((nb, POOL_STATE, pw), lambda bi, ti: (bi, 0, 0)))
        args.append(prefix)
    in_specs += [pl.BlockSpec((tm, mw), rows),
                 pl.BlockSpec((tm, d), lambda bi, ti: (bi * nt + ti, 4 * mw // d)),
                 pl.BlockSpec((tm, d), lambda bi, ti: (bi * nt + ti, 4 * mw // d + 1)),
                 pl.BlockSpec((nb, lt, d), seq3),
                 pl.BlockSpec((nb, None, 1, d), modk(2)),
                 pl.BlockSpec((nb, None, 1, d), modk(3)),
                 pl.BlockSpec((nb, None, 1, d), modk(4)),
                 pl.BlockSpec(w_pool.shape, lambda bi, ti: (0, 0, 0), pipeline_mode=single),
                 pl.BlockSpec((1, pw), const2),
                 pl.BlockSpec(w_pa.shape, const2, pipeline_mode=single),
                 pl.BlockSpec(w_pb.shape, const2, pipeline_mode=single),
                 pl.BlockSpec(w_out.shape, const2, pipeline_mode=single),
                 pl.BlockSpec((1, d), const2),
                 pl.BlockSpec((1, d), const2)]
    args += [b_out, big, big, x, mod4, mod4, mod4, w_pool, pool_scale, w_pa, w_pb, w_out, ln1_g, ln1_b]
    wbytes = 2 * (w_pool.size + w_pa.size + w_pb.size + w_out.size)
    nbytes = (wbytes + 2 * tm * (pw * 4 + mw * 2 + 2 * d * 2 + d * 4) + 2 * tm * (d * 4 + d * 2)
              + nb * (POOL_HALO + lt) * pw * 4 + 6 * tm * d * 4)
    x1, u2 = pl.pallas_call(
        functools.partial(_mix_kernel, start_pos=start_pos, alpha=alpha, has_prefix=has_prefix),
        grid=(b // nb, nt),
        in_specs=in_specs,
        out_specs=[pl.BlockSpec((nb, lt, d), seq3), pl.BlockSpec((tm, d), rows)],
        out_shape=[jax.ShapeDtypeStruct((b, l, d), F32), jax.ShapeDtypeStruct((b * l, d), BF16)],
        scratch_shapes=[pltpu.VMEM((nb, POOL_HALO + lt, pw), F32)],
        compiler_params=_params(nbytes, 2),
        name="mix",
    )(*args)
    return x1, u2


def _ffn_kernel(u_ref, wg_ref, wu_ref, wd_ref, x1_ref, g2_ref, l2g_ref, l2b_ref, y_ref, acc_ref, *, alpha):
    nb, lt, d = x1_ref.shape
    j = pl.program_id(1)
    u = u_ref[...]
    hg = _dot(u, wg_ref[...])
    hu = _dot(u, wu_ref[...])
    hidden = (hg * jax.nn.sigmoid(hg) * hu).astype(BF16)
    part = _dot(hidden, wd_ref[...])

    @pl.when(j == 0)
    def _():
        acc_ref[...] = part

    @pl.when(j > 0)
    def _():
        acc_ref[...] += part

    @pl.when(j == pl.num_programs(1) - 1)
    def _():
        f = acc_ref[...].reshape(nb, lt, d)
        z = alpha * x1_ref[...] + g2_ref[...] * f
        y_ref[...] = _layer_norm(z) * l2g_ref[...] + l2b_ref[...]


def _ffn(u2, x1, mod4, w_gate, w_up, w_down, ln2_g, ln2_b, nb, lt, tf, alpha):
    b, l, d = x1.shape
    ff = w_gate.shape[1]
    tm = nb * lt
    nt = l // lt
    const2 = lambda i, j: (0, 0)
    seq3 = lambda i, j: (i // nt, i % nt, 0)
    nbytes = (2 * tm * d * 2 + 2 * 3 * d * tf * 2 + 4 * tm * d * 4 + tm * d * 4
              + 3 * tm * tf * 4 + 2 * tm * d * 4)
    return pl.pallas_call(
        functools.partial(_ffn_kernel, alpha=alpha),
        grid=((b // nb) * nt, ff // tf),
        in_specs=[pl.BlockSpec((tm, d), lambda i, j: (i, 0)),
                  pl.BlockSpec((d, tf), lambda i, j: (0, j)),
                  pl.BlockSpec((d, tf), lambda i, j: (0, j)),
                  pl.BlockSpec((tf, d), lambda i, j: (j, 0)),
                  pl.BlockSpec((nb, lt, d), seq3),
                  pl.BlockSpec((nb, None, 1, d), lambda i, j: (i // nt, 5, 0, 0)),
                  pl.BlockSpec((1, d), const2),
                  pl.BlockSpec((1, d), const2)],
        out_specs=pl.BlockSpec((nb, lt, d), seq3),
        out_shape=jax.ShapeDtypeStruct((b, l, d), F32),
        scratch_shapes=[pltpu.VMEM((tm, d), F32)],
        compiler_params=_params(nbytes, 2),
        name="ffn",
    )(u2, w_gate, w_up, w_down, x1, mod4, ln2_g, ln2_b)


def _tile(total, want):
    if total <= want:
        return total
    t = want
    while total % t or t % 8:
        t -= 1
    return t


def _stream(x, mod, state, start_pos, w, alpha):
    b, l, d = x.shape
    nh = w["nh"]
    mod4 = mod.reshape(b, 6, 1, d)

    def tiling(rows_want):
        lt = _tile(l, rows_want)
        nb = _tile(b, max(1, rows_want // lt)) if lt == l else 1
        return nb, lt

    nb, lt = tiling(512)
    u, p, ifg = _inproj(x, mod4, w["w_p"], w["w_if"], nb, lt)
    tm = _tile(b * l, 1024)
    big = _matmul(u, w["w_big"], tm, _tile(w["w_big"].shape[1], 1024))

    lc = _tile(l, 256)
    mstate = None if state is None else state[1:]
    b_out, c_new, n_new, m_new = _mlstm(big, ifg, w["gate_bias"], w["gn_w"], mstate, b, l, lc, nh)

    nb, lt = tiling(256)
    prefix = None if state is None else state[0]
    x1, u2 = _mix(p, prefix, b_out, big, x, mod4, w["w_pool"], w["pool_scale"], w["w_pa"], w["w_pb"],
                  w["w_out"], w["ln1_g"], w["ln1_b"], nb, lt, start_pos, alpha)

    nb, lt = tiling(512)
    ff = w["w_gate"].shape[1]
    tf = 512 if ff % 512 == 0 else ff
    y = _ffn(u2, x1, mod4, w["w_gate"], w["w_up"], w["w_down"], w["ln2_g"], w["ln2_b"], nb, lt, tf, alpha)
    pool_state = p.reshape(b, l, -1)[:, l - POOL_STATE:, :]
    return y, pool_state, c_new, n_new, m_new


def kernel(x_prompt, x_sample, c_prompt, c_sample, state_pool, state_mlstm_C, state_mlstm_n, state_mlstm_m, w_ada, b_ada, w_in, b_i, b_f, w_pool, pool_scale, gn_w, w_pa, w_pb, w_out, ln1_g, ln1_b, w_gate, w_up, w_down, ln2_g, ln2_b):
    depth = w_ada.shape[0]
    alpha = (2 * depth) ** 0.25
    d = x_prompt.shape[2]
    nh = b_i.shape[1]
    pw = pool_scale.shape[1]
    mw = gn_w.shape[1]
    nbp = x_prompt.shape[0]
    y_p, y_s = x_prompt, x_sample
    outs_p, outs_s = [], []
    for li in range(depth):
        wl = w_in[li]
        gate_cols = wl[:, pw + 4 * mw:pw + 4 * mw + 2 * nh]
        w = {
            "nh": nh,
            "w_p": wl[:, :pw].astype(BF16),
            "w_if": jnp.pad(gate_cols, ((0, 0), (0, LANES - 2 * nh))).astype(BF16),
            "w_big": jnp.concatenate([wl[:, pw:pw + 4 * mw], wl[:, pw + 4 * mw + 2 * nh:]], axis=1).astype(BF16),
            "gate_bias": jnp.pad(jnp.concatenate([b_i[li], b_f[li]]), (0, LANES - 2 * nh)).reshape(1, LANES),
            "gn_w": gn_w[li].reshape(1, mw),
            "w_pool": w_pool[li].astype(BF16),
            "pool_scale": pool_scale[li].reshape(1, pw),
            "w_pa": w_pa[li].astype(BF16),
            "w_pb": w_pb[li].astype(BF16),
            "w_out": w_out[li].astype(BF16),
            "ln1_g": ln1_g[li].reshape(1, d),
            "ln1_b": ln1_b[li].reshape(1, d),
            "w_gate": w_gate[li].astype(BF16),
            "w_up": w_up[li].astype(BF16),
            "w_down": w_down[li].astype(BF16),
            "ln2_g": ln2_g[li].reshape(1, d),
            "ln2_b": ln2_b[li].reshape(1, d),
        }
        mod = _ada(jnp.concatenate([c_prompt, c_sample], axis=0), w_ada[li], b_ada[li])
        res_p = _stream(y_p, mod[:nbp], None, 0, w, alpha)
        res_s = _stream(y_s, mod[nbp:], (state_pool[li], state_mlstm_C[li], state_mlstm_n[li], state_mlstm_m[li]),
                        PAST_LEN, w, alpha)
        y_p, y_s = res_p[0], res_s[0]
        outs_p.append(res_p[1:])
        outs_s.append(res_s[1:])
    stack = lambda outs, k: jnp.stack([o[k] for o in outs], axis=0)
    return (y_p, y_s,
            stack(outs_p, 0), stack(outs_p, 1), stack(outs_p, 2), stack(outs_p, 3),
            stack(outs_s, 0), stack(outs_s, 1), stack(outs_s, 2), stack(outs_s, 3))
```

```python
import functools

import jax
import jax.numpy as jnp
from jax import lax
from jax.experimental import pallas as pl
from jax.experimental.pallas import tpu as pltpu

F32 = jnp.float32
BF16 = jnp.bfloat16

LN_EPS = 1e-5
POOL_WINDOWS = (2, 4, 8, 16)
POOL_STATE = max(POOL_WINDOWS) - 1
POOL_HALO = 16
PAST_LEN = 1024
LANES = 128
VMEM_CAP_BYTES = 60 * 1024 * 1024


def _vmem_limit(nbytes):
    return int(min(VMEM_CAP_BYTES, max(32 * 1024 * 1024, nbytes * 5 // 4)))


def _params(nbytes, ndims):
    return pltpu.CompilerParams(dimension_semantics=("arbitrary",) * ndims,
                                vmem_limit_bytes=_vmem_limit(nbytes))


def _layer_norm(x):
    mu = jnp.mean(x, axis=-1, keepdims=True)
    xc = x - mu
    var = jnp.mean(xc * xc, axis=-1, keepdims=True)
    return xc * lax.rsqrt(var + LN_EPS)


def _dot(a, b):
    return jnp.dot(a, b, preferred_element_type=F32)


def _log_sigmoid(x):
    return jnp.minimum(x, 0.0) - jnp.log1p(jnp.exp(-jnp.abs(x)))


def _split3(x):
    hi = x.astype(BF16)
    r1 = x - hi.astype(F32)
    mid = r1.astype(BF16)
    lo = (r1 - mid.astype(F32)).astype(BF16)
    return hi, mid, lo


def _ada_kernel(c_ref, w_ref, b_ref, o_ref):
    c = c_ref[...]
    a = (c * jax.nn.sigmoid(c)).astype(BF16)
    o_ref[...] = _dot(a, w_ref[...].astype(BF16)) + b_ref[...]


def _ada(c, w_ada, b_ada):
    nb, d = c.shape
    n = w_ada.shape[1]
    tn = 1024 if n % 1024 == 0 else n
    nbytes = 2 * (d * tn * 4) + d * tn * 2 + 4 * nb * (d + tn) * 4
    return pl.pallas_call(
        _ada_kernel,
        grid=(n // tn,),
        in_specs=[pl.BlockSpec((nb, d), lambda j: (0, 0)),
                  pl.BlockSpec((d, tn), lambda j: (0, j)),
                  pl.BlockSpec((1, tn), lambda j: (0, j))],
        out_specs=pl.BlockSpec((nb, tn), lambda j: (0, j)),
        out_shape=jax.ShapeDtypeStruct((nb, n), F32),
        compiler_params=_params(nbytes, 1),
        name="ada",
    )(c, w_ada, b_ada.reshape(1, n))


def _inproj_kernel(x_ref, sh_ref, sc_ref, wp_ref, wif_ref, u_ref, p_ref, ifg_ref):
    nb, lt, d = x_ref.shape
    u = _layer_norm(x_ref[...]) * (1.0 + sc_ref[...]) + sh_ref[...]
    ub = u.reshape(nb * lt, d).astype(BF16)
    u_ref[...] = ub
    p_ref[...] = _dot(ub, wp_ref[...])
    ifg_ref[...] = _dot(ub, wif_ref[...])


def _inproj(x, mod4, w_front, pw, w_if, nb, lt):
    b, l, d = x.shape
    tm = nb * lt
    nt = l // lt
    rows = lambda bi, ti: (bi * nt + ti, 0)
    nbytes = (2 * tm * d * 4 + 2 * tm * d * 2 + 2 * tm * pw * 4 + 2 * d * pw * 2
              + 3 * tm * d * 4 + tm * pw * 4)
    return pl.pallas_call(
        _inproj_kernel,
        grid=(b // nb, nt),
        in_specs=[pl.BlockSpec((nb, lt, d), lambda bi, ti: (bi, ti, 0)),
                  pl.BlockSpec((nb, None, 1, d), lambda bi, ti: (bi, 0, 0, 0)),
                  pl.BlockSpec((nb, None, 1, d), lambda bi, ti: (bi, 1, 0, 0)),
                  pl.BlockSpec((d, pw), lambda bi, ti: (0, 0)),
                  pl.BlockSpec((d, LANES), lambda bi, ti: (0, 0))],
        out_specs=[pl.BlockSpec((tm, d), rows),
                   pl.BlockSpec((tm, pw), rows),
                   pl.BlockSpec((tm, LANES), rows)],
        out_shape=[jax.ShapeDtypeStruct((b * l, d), BF16),
                   jax.ShapeDtypeStruct((b * l, pw), F32),
                   jax.ShapeDtypeStruct((b * l, LANES), F32)],
        compiler_params=_params(nbytes, 2),
        name="inproj",
    )(x, mod4, mod4, w_front, w_if)


def _matmul_kernel(a_ref, b_ref, o_ref):
    o_ref[...] = _dot(a_ref[...], b_ref[...]).astype(o_ref.dtype)


def _matmul(a, b, col0, n, tm, tn):
    m, k = a.shape
    j0 = col0 // tn
    nbytes = 2 * tm * k * 2 + 2 * k * tn * 2 + 2 * tm * tn * 2 + tm * tn * 4
    return pl.pallas_call(
        _matmul_kernel,
        grid=(m // tm, n // tn),
        in_specs=[pl.BlockSpec((tm, k), lambda i, j: (i, 0)),
                  pl.BlockSpec((k, tn), lambda i, j: (0, j + j0))],
        out_specs=pl.BlockSpec((tm, tn), lambda i, j: (i, j)),
        out_shape=jax.ShapeDtypeStruct((m, n), BF16),
        compiler_params=_params(nbytes, 2),
        name="bigproj",
    )(a, b)


def _mlstm_kernel(*refs, nh, hd, zero_init):
    if zero_init:
        (q_ref, k_ref, v_ref, o_ref, ifg_ref, bias_ref, gnw_ref,
         h_ref, c_out, n_out, m_out, c_scr, m_scr, d_scr, f_scr) = refs
    else:
        (q_ref, k_ref, v_ref, o_ref, ifg_ref, bias_ref, gnw_ref, c0_ref, n0_ref, m0_ref,
         h_ref, c_out, n_out, m_out, c_scr, m_scr, d_scr, f_scr) = refs
    lc = q_ref.shape[0]
    ci = pl.program_id(1)

    @pl.when(ci == 0)
    def _():
        if zero_init:
            c_scr[...] = jnp.zeros_like(c_scr)
            m_scr[...] = jnp.zeros_like(m_scr)
        else:
            c_scr[:, :, 0:hd] = c0_ref[0]
            c_scr[:, :, hd:hd + LANES] = n0_ref[0]
            m_scr[...] = m0_ref[0]

    gates = ifg_ref[...] + bias_ref[...]
    lf = _log_sigmoid(gates)
    row_t = lax.broadcasted_iota(jnp.int32, (lc, lc), 0)
    col_s = lax.broadcasted_iota(jnp.int32, (lc, lc), 1)
    causal = col_s <= row_t
    tril = jnp.where(causal, 1.0, 0.0).astype(BF16)
    b_all = sum(_dot(tril, piece) for piece in _split3(lf))
    b_col = pltpu.roll(b_all, LANES - nh, 1)
    g_col = gates - b_col
    g_row = jnp.transpose(g_col)
    scale = hd ** -0.5
    scale_is_pow2 = (hd & (hd - 1)) == 0 and (hd.bit_length() - 1) % 2 == 0
    ones_v = jnp.ones((lc, LANES), BF16)
    reps = hd // LANES
    wide = lambda col: jnp.concatenate([col] * reps, axis=1)

    m_heads, decays = [], []
    for h in range(nh):
        m_prev = m_scr[h]
        x = jnp.where(causal, g_row[h:h + 1, :], -jnp.inf)
        m_run = jnp.maximum(jnp.max(x, axis=1, keepdims=True), m_prev)
        m_run_b = jnp.broadcast_to(m_run, (lc, LANES))
        if lc % LANES == 0:
            d_scr[h] = jnp.exp(x - jnp.concatenate([m_run_b] * (lc // LANES), axis=1))
        else:
            d_scr[h] = jnp.exp(x - m_run)
        m_last = m_run[lc - 1:lc, :]
        m_row = jnp.broadcast_to(b_col[:, h:h + 1], (lc, LANES)) + m_run_b
        f_scr[h, 0] = jnp.exp(m_prev - m_run_b)
        f_scr[h, 1] = jnp.exp(-m_row)
        f_scr[h, 2] = jnp.exp(jnp.broadcast_to(g_col[:, h:h + 1], (lc, LANES)) - m_last)
        decays.append(jnp.exp(m_prev - m_last))
        m_heads.append(b_col[lc - 1:lc, h:h + 1] + m_last)

    h_heads, c_heads = [], []
    for h in range(nh):
        sl = slice(h * hd, (h + 1) * hd)
        q = q_ref[:, sl]
        if scale_is_pow2:
            k = k_ref[:, sl] * scale
        else:
            k = (k_ref[:, sl].astype(F32) * scale).astype(BF16)
        v_aug = jnp.concatenate([v_ref[:, sl], ones_v], axis=1)
        c_prev = c_scr[h]

        qk = lax.dot_general(q, k, (((1,), (1,)), ((), ())), preferred_element_type=F32)
        s = (qk * d_scr[h]).astype(BF16)
        qw = (q.astype(F32) * wide(f_scr[h, 0])).astype(BF16)
        tot = _dot(jnp.concatenate([qw, s], axis=1),
                   jnp.concatenate([c_prev.astype(BF16), v_aug], axis=0))
        inv = 1.0 / jnp.maximum(jnp.abs(tot[:, hd:]), f_scr[h, 1])
        hg = tot[:, :hd] * wide(inv) * jax.nn.sigmoid(o_ref[:, sl].astype(F32))
        h_heads.append((_layer_norm(hg) * gnw_ref[:, sl]).astype(BF16))

        kw = (k.astype(F32) * wide(f_scr[h, 2])).astype(BF16)
        c_heads.append(decays[h] * c_prev + lax.dot_general(
            kw, v_aug, (((0,), (0,)), ((), ())), preferred_element_type=F32))

    h_ref[...] = jnp.concatenate(h_heads, axis=1)
    c_scr[...] = jnp.stack(c_heads, axis=0)
    m_scr[...] = jnp.stack(m_heads, axis=0)

    @pl.when(ci == pl.num_programs(1) - 1)
    def _():
        c_out[0] = c_scr[:, :, 0:hd]
        n_out[0] = c_scr[:, :, hd:hd + LANES]
        m_out[0] = m_scr[...]


def _mlstm(qkvo, ifg, gate_bias, gn_w, state, b, l, lc, nh):
    mw = gn_w.shape[1]
    hd = mw // nh
    assert hd % LANES == 0
    nc = l // lc
    rows = lambda bi, ci: (bi * nc + ci, 0)
    col = lambda j: (lambda bi, ci: (bi * nc + ci, j))
    st4 = lambda bi, ci: (bi, 0, 0, 0)
    zero_init = state is None
    in_specs = [pl.BlockSpec((lc, mw), col(0)), pl.BlockSpec((lc, mw), col(1)),
                pl.BlockSpec((lc, mw), col(2)), pl.BlockSpec((lc, mw), col(3)),
                pl.BlockSpec((lc, LANES), rows),
                pl.BlockSpec((1, LANES), lambda bi, ci: (0, 0)),
                pl.BlockSpec((1, mw), lambda bi, ci: (0, 0))]
    args = [qkvo, qkvo, qkvo, qkvo, ifg, gate_bias, gn_w]
    if not zero_init:
        c0, n0, m0 = state
        in_specs += [pl.BlockSpec((1, nh, hd, hd), st4), pl.BlockSpec((1, nh, hd, LANES), st4),
                     pl.BlockSpec((1, nh, 1, 1), st4)]
        args += [c0, jnp.broadcast_to(n0[..., None], (b, nh, hd, LANES)), m0.reshape(b, nh, 1, 1)]
    nbytes = (2 * 5 * lc * mw * 2 + 5 * nh * hd * (hd + LANES) * 4 + nh * lc * (lc + 3 * LANES) * 4
              + 16 * lc * max(lc, hd + LANES) * 4)
    h, c, n, m = pl.pallas_call(
        functools.partial(_mlstm_kernel, nh=nh, hd=hd, zero_init=zero_init),
        grid=(b, nc),
        in_specs=in_specs,
        out_specs=[pl.BlockSpec((lc, mw), rows), pl.BlockSpec((1, nh, hd, hd), st4),
                   pl.BlockSpec((1, nh, hd, LANES), st4), pl.BlockSpec((1, nh, 1, 1), st4)],
        out_shape=[jax.ShapeDtypeStruct((b * l, mw), BF16),
                   jax.ShapeDtypeStruct((b, nh, hd, hd), F32),
                   jax.ShapeDtypeStruct((b, nh, hd, LANES), F32),
                   jax.ShapeDtypeStruct((b, nh, 1, 1), F32)],
        scratch_shapes=[pltpu.VMEM((nh, hd, hd + LANES), F32), pltpu.VMEM((nh, 1, 1), F32),
                        pltpu.VMEM((nh, lc, lc), F32), pltpu.VMEM((nh, 3, lc, LANES), F32)],
        compiler_params=_params(nbytes, 2),
        name="mlstm",
    )(*args)
    return h, c, n[..., 0], m.reshape(b, nh)


def _mix_kernel(*refs, start_pos, alpha, has_prefix):
    if has_prefix:
        (p_ref, pre_ref, bo_ref, ga_ref, gb_ref, x_ref, g1_ref, sh2_ref, sc2_ref, wpool_ref, psc_ref,
         wpa_ref, wpb_ref, wout_ref, l1g_ref, l1b_ref, x1_ref, u2_ref, ext_scr) = refs
    else:
        (p_ref, bo_ref, ga_ref, gb_ref, x_ref, g1_ref, sh2_ref, sc2_ref, wpool_ref, psc_ref,
         wpa_ref, wpb_ref, wout_ref, l1g_ref, l1b_ref, x1_ref, u2_ref, ext_scr) = refs
    nb, lt, pw = p_ref.shape
    d = x_ref.shape[2]
    grp = pw // len(POOL_WINDOWS)
    ti = pl.program_id(1)

    @pl.when(ti == 0)
    def _():
        ext_scr[:, 0:POOL_HALO, :] = jnp.zeros((nb, POOL_HALO, pw), F32)
        if has_prefix:
            ext_scr[:, POOL_HALO - POOL_STATE:POOL_HALO, :] = pre_ref[...]

    @pl.when(ti > 0)
    def _():
        ext_scr[:, 0:POOL_HALO, :] = ext_scr[:, lt:lt + POOL_HALO, :]

    ext_scr[:, POOL_HALO:POOL_HALO + lt, :] = p_ref[...]

    pos = start_pos + ti * lt + lax.broadcasted_iota(jnp.int32, (lt, LANES), 0)
    branch = []
    for g, w in enumerate(POOL_WINDOWS):
        cs = slice(g * grp, (g + 1) * grp)
        inv = 1.0 / jnp.minimum(pos + 1, w).astype(F32)
        inv = jnp.concatenate([inv] * (grp // LANES), axis=1)
        ys = []
        for bi in range(nb):
            ext = ext_scr[bi, :, cs]
            acc = ext
            shift = 1
            while shift < w:
                acc = acc + pltpu.roll(acc, shift, 0)
                shift *= 2
            ys.append(acc[POOL_HALO:] * inv - ext[POOL_HALO:])
        y = jnp.concatenate(ys, axis=0).astype(BF16)
        branch.append((_dot(y, wpool_ref[g]) * psc_ref[:, cs]).astype(BF16))
    a_out = jnp.concatenate(branch, axis=1)

    pa = _dot(a_out, wpa_ref[...])
    pb = _dot(bo_ref[...], wpb_ref[...])
    merged = (jax.nn.sigmoid(ga_ref[...].astype(F32)) * pa
              + jax.nn.sigmoid(gb_ref[...].astype(F32)) * pb).astype(BF16)
    t = _dot(merged, wout_ref[...]).reshape(nb, lt, d)
    x1 = _layer_norm(alpha * x_ref[...] + g1_ref[...] * t) * l1g_ref[...] + l1b_ref[...]
    x1_ref[...] = x1
    u2 = _layer_norm(x1) * (1.0 + sc2_ref[...]) + sh2_ref[...]
    u2_ref[...] = u2.reshape(nb * lt, d).astype(BF16)


def _mix(p, prefix, b_out, gg, x, mod4, w_pool, pool_scale, w_pa, w_pb, w_out, ln1_g, ln1_b,
         nb, lt, start_pos, alpha):
    b, l, d = x.shape
    pw = p.shape[1]
    mw = b_out.shape[1]
    assert (pw // len(POOL_WINDOWS)) % LANES == 0
    tm = nb * lt
    nt = l // lt
    rows = lambda bi, ti: (bi * nt + ti, 0)
    seq3 = lambda bi, ti: (bi, ti, 0)
    const2 = lambda bi, ti: (0, 0)
    modk = lambda kk: (lambda bi, ti: (bi, kk, 0, 0))
    has_prefix = prefix is not None
    single = pl.Buffered(1)
    in_specs = [pl.BlockSpec((nb, lt, pw), seq3)]
    args = [p.reshape(b, l, pw)]
    if has_prefix:
        in_specs.append(pl.BlockSpec((nb, POOL_STATE, pw), lambda bi, ti: (bi, 0, 0)))
        args.append(prefix)
    in_specs += [pl.BlockSpec((tm, mw), rows),
                 pl.BlockSpec((tm, d), rows),
                 pl.BlockSpec((tm, d), lambda bi, ti: (bi * nt + ti, 1)),
                 pl.BlockSpec((nb, lt, d), seq3),
                 pl.BlockSpec((nb, None, 1, d), modk(2)),
                 pl.BlockSpec((nb, None, 1, d), modk(3)),
                 pl.BlockSpec((nb, None, 1, d), modk(4)),
                 pl.BlockSpec(w_pool.shape, lambda bi, ti: (0, 0, 0), pipeline_mode=single),
                 pl.BlockSpec((1, pw), const2),
                 pl.BlockSpec(w_pa.shape, const2, pipeline_mode=single),
                 pl.BlockSpec(w_pb.shape, const2, pipeline_mode=single),
                 pl.BlockSpec(w_out.shape, const2, pipeline_mode=single),
                 pl.BlockSpec((1, d), const2),
                 pl.BlockSpec((1, d), const2)]
    args += [b_out, gg, gg, x, mod4, mod4, mod4, w_pool, pool_scale, w_pa, w_pb, w_out, ln1_g, ln1_b]
    wbytes = 2 * (w_pool.size + w_pa.size + w_pb.size + w_out.size)
    nbytes = (wbytes + 2 * tm * (pw * 4 + mw * 2 + 2 * d * 2 + d * 4) + 2 * tm * (d * 4 + d * 2)
              + nb * (POOL_HALO + lt) * pw * 4 + 6 * tm * d * 4)
    x1, u2 = pl.pallas_call(
        functools.partial(_mix_kernel, start_pos=start_pos, alpha=alpha, has_prefix=has_prefix),
        grid=(b // nb, nt),
        in_specs=in_specs,
        out_specs=[pl.BlockSpec((nb, lt, d), seq3), pl.BlockSpec((tm, d), rows)],
        out_shape=[jax.ShapeDtypeStruct((b, l, d), F32), jax.ShapeDtypeStruct((b * l, d), BF16)],
        scratch_shapes=[pltpu.VMEM((nb, POOL_HALO + lt, pw), F32)],
        compiler_params=_params(nbytes, 2),
        name="mix",
    )(*args)
    return x1, u2


def _ffn_up_kernel(u_ref, wg_ref, wu_ref, h_ref):
    u = u_ref[...]
    hg = _dot(u, wg_ref[...])
    hu = _dot(u, wu_ref[...])
    h_ref[...] = (hg * jax.nn.sigmoid(hg) * hu).astype(BF16)


def _ffn_up(u2, w_gate, w_up, tm, tn):
    m, d = u2.shape
    ff = w_gate.shape[1]
    nbytes = 2 * tm * d * 2 + 2 * 2 * d * tn * 2 + 2 * tm * tn * 2 + 4 * tm * tn * 4
    return pl.pallas_call(
        _ffn_up_kernel,
        grid=(m // tm, ff // tn),
        in_specs=[pl.BlockSpec((tm, d), lambda i, j: (i, 0)),
                  pl.BlockSpec((d, tn), lambda i, j: (0, j)),
                  pl.BlockSpec((d, tn), lambda i, j: (0, j))],
        out_specs=pl.BlockSpec((tm, tn), lambda i, j: (i, j)),
        out_shape=jax.ShapeDtypeStruct((m, ff), BF16),
        compiler_params=_params(nbytes, 2),
        name="ffn_up",
    )(u2, w_gate, w_up)


def _ffn_down_kernel(h_ref, wd_ref, x1_ref, g2_ref, l2g_ref, l2b_ref, y_ref, *, alpha):
    nb, lt, d = x1_ref.shape
    f = _dot(h_ref[...], wd_ref[...]).reshape(nb, lt, d)
    z = alpha * x1_ref[...] + g2_ref[...] * f
    y_ref[...] = _layer_norm(z) * l2g_ref[...] + l2b_ref[...]


def _ffn_down(hidden, x1, mod4, w_down, ln2_g, ln2_b, nb, lt, alpha):
    b, l, d = x1.shape
    ff = w_down.shape[0]
    tm = nb * lt
    nt = l // lt
    const2 = lambda bi, ti: (0, 0)
    seq3 = lambda bi, ti: (bi, ti, 0)
    nbytes = 2 * tm * ff * 2 + ff * d * 2 + 4 * tm * d * 4 + 4 * tm * d * 4
    return pl.pallas_call(
        functools.partial(_ffn_down_kernel, alpha=alpha),
        grid=(b // nb, nt),
        in_specs=[pl.BlockSpec((tm, ff), lambda bi, ti: (bi * nt + ti, 0)),
                  pl.BlockSpec((ff, d), const2, pipeline_mode=pl.Buffered(1)),
                  pl.BlockSpec((nb, lt, d), seq3),
                  pl.BlockSpec((nb, None, 1, d), lambda bi, ti: (bi, 5, 0, 0)),
                  pl.BlockSpec((1, d), const2),
                  pl.BlockSpec((1, d), const2)],
        out_specs=pl.BlockSpec((nb, lt, d), seq3),
        out_shape=jax.ShapeDtypeStruct((b, l, d), F32),
        compiler_params=_params(nbytes, 2),
        name="ffn_down",
    )(hidden, w_down, x1, mod4, ln2_g, ln2_b)


def _tile(total, want):
    if total <= want:
        return total
    t = want
    while total % t or t % 8:
        t -= 1
    return t


def _stream(x, mod, state, start_pos, w, alpha):
    b, l, d = x.shape
    nh = w["nh"]
    pw = w["pool_scale"].shape[1]
    mw = w["gn_w"].shape[1]
    mod4 = mod.reshape(b, 6, 1, d)

    def tiling(rows_want):
        lt = _tile(l, rows_want)
        nb = _tile(b, max(1, rows_want // lt)) if lt == l else 1
        return nb, lt

    nb, lt = tiling(512)
    u, p, ifg = _inproj(x, mod4, w["w_front"], pw, w["w_if"], nb, lt)
    tm = _tile(b * l, 1024)
    qkvo = _matmul(u, w["w_front"], pw, 4 * mw, tm, 1024)
    gg = _matmul(u, w["w_gg"], 0, 2 * d, tm, 1024)

    lc = _tile(l, 256)
    mstate = None if state is None else state[1:]
    b_out, c_new, n_new, m_new = _mlstm(qkvo, ifg, w["gate_bias"], w["gn_w"], mstate, b, l, lc, nh)

    nb, lt = tiling(256)
    prefix = None if state is None else state[0]
    x1, u2 = _mix(p, prefix, b_out, gg, x, mod4, w["w_pool"], w["pool_scale"], w["w_pa"], w["w_pb"],
                  w["w_out"], w["ln1_g"], w["ln1_b"], nb, lt, start_pos, alpha)

    ff = w["w_gate"].shape[1]
    hidden = _ffn_up(u2, w["w_gate"], w["w_up"], tm, 512 if ff % 512 == 0 else ff)
    y = _ffn_down(hidden, x1, mod4, w["w_down"], w["ln2_g"], w["ln2_b"], nb, lt, alpha)
    pool_state = p.reshape(b, l, pw)[:, l - POOL_STATE:, :]
    return y, pool_state, c_new, n_new, m_new


def kernel(x_prompt, x_sample, c_prompt, c_sample, state_pool, state_mlstm_C, state_mlstm_n, state_mlstm_m, w_ada, b_ada, w_in, b_i, b_f, w_pool, pool_scale, gn_w, w_pa, w_pb, w_out, ln1_g, ln1_b, w_gate, w_up, w_down, ln2_g, ln2_b):
    depth = w_ada.shape[0]
    alpha = (2 * depth) ** 0.25
    d = x_prompt.shape[2]
    nh = b_i.shape[1]
    pw = pool_scale.shape[1]
    mw = gn_w.shape[1]
    nbp = x_prompt.shape[0]
    assert pw % 1024 == 0 and mw % 1024 == 0 and d % 1024 == 0
    y_p, y_s = x_prompt, x_sample
    outs_p, outs_s = [], []
    for li in range(depth):
        wl = w_in[li]
        gates0 = pw + 4 * mw
        w = {
            "nh": nh,
            "w_front": wl[:, :gates0].astype(BF16),
            "w_if": jnp.pad(wl[:, gates0:gates0 + 2 * nh], ((0, 0), (0, LANES - 2 * nh))).astype(BF16),
            "w_gg": wl[:, gates0 + 2 * nh:].astype(BF16),
            "gate_bias": jnp.pad(jnp.concatenate([b_i[li], b_f[li]]), (0, LANES - 2 * nh)).reshape(1, LANES),
            "gn_w": gn_w[li].reshape(1, mw),
            "w_pool": w_pool[li].astype(BF16),
            "pool_scale": pool_scale[li].reshape(1, pw),
            "w_pa": w_pa[li].astype(BF16),
            "w_pb": w_pb[li].astype(BF16),
            "w_out": w_out[li].astype(BF16),
            "ln1_g": ln1_g[li].reshape(1, d),
            "ln1_b": ln1_b[li].reshape(1, d),
            "w_gate": w_gate[li].astype(BF16),
            "w_up": w_up[li].astype(BF16),
            "w_down": w_down[li].astype(BF16),
            "ln2_g": ln2_g[li].reshape(1, d),
            "ln2_b": ln2_b[li].reshape(1, d),
        }
        mod = _ada(jnp.concatenate([c_prompt, c_sample], axis=0), w_ada[li], b_ada[li])
        res_p = _stream(y_p, mod[:nbp], None, 0, w, alpha)
        res_s = _stream(y_s, mod[nbp:], (state_pool[li], state_mlstm_C[li], state_mlstm_n[li], state_mlstm_m[li]),
                        PAST_LEN, w, alpha)
        y_p, y_s = res_p[0], res_s[0]
        outs_p.append(res_p[1:])
        outs_s.append(res_s[1:])
    stack = lambda outs, k: jnp.stack([o[k] for o in outs], axis=0)
    return (y_p, y_s,
            stack(outs_p, 0), stack(outs_p, 1), stack(outs_p, 2), stack(outs_p, 3),
            stack(outs_s, 0), stack(outs_s, 1), stack(outs_s, 2), stack(outs_s, 3))
```

```python
import functools

import jax
import jax.numpy as jnp
from jax import lax
from jax.experimental import pallas as pl
from jax.experimental.pallas import tpu as pltpu

F32 = jnp.float32
BF16 = jnp.bfloat16

LN_EPS = 1e-5
POOL_WINDOWS = (2, 4, 8, 16)
POOL_STATE = max(POOL_WINDOWS) - 1
POOL_HALO = 16
PAST_LEN = 1024
LANES = 128
VMEM_CAP_BYTES = 60 * 1024 * 1024


def _vmem_limit(nbytes):
    return int(min(VMEM_CAP_BYTES, max(32 * 1024 * 1024, nbytes * 5 // 4)))


def _params(nbytes, ndims):
    return pltpu.CompilerParams(dimension_semantics=("arbitrary",) * ndims,
                                vmem_limit_bytes=_vmem_limit(nbytes))


def _layer_norm(x):
    mu = jnp.mean(x, axis=-1, keepdims=True)
    xc = x - mu
    var = jnp.mean(xc * xc, axis=-1, keepdims=True)
    return xc * lax.rsqrt(var + LN_EPS)


def _dot(a, b):
    return jnp.dot(a, b, preferred_element_type=F32)


def _log_sigmoid(x):
    return jnp.minimum(x, 0.0) - jnp.log1p(jnp.exp(-jnp.abs(x)))


def _split3(x):
    hi = x.astype(BF16)
    r1 = x - hi.astype(F32)
    mid = r1.astype(BF16)
    lo = (r1 - mid.astype(F32)).astype(BF16)
    return hi, mid, lo


def _ada_kernel(c_ref, w_ref, b_ref, o_ref):
    c = c_ref[...]
    a = (c * jax.nn.sigmoid(c)).astype(BF16)
    o_ref[...] = _dot(a, w_ref[...].astype(BF16)) + b_ref[...]


def _ada(c, w_ada, b_ada):
    nb, d = c.shape
    n = w_ada.shape[1]
    tn = 1024 if n % 1024 == 0 else n
    nbytes = 2 * (d * tn * 4) + d * tn * 2 + 4 * nb * (d + tn) * 4
    return pl.pallas_call(
        _ada_kernel,
        grid=(n // tn,),
        in_specs=[pl.BlockSpec((nb, d), lambda j: (0, 0)),
                  pl.BlockSpec((d, tn), lambda j: (0, j)),
                  pl.BlockSpec((1, tn), lambda j: (0, j))],
        out_specs=pl.BlockSpec((nb, tn), lambda j: (0, j)),
        out_shape=jax.ShapeDtypeStruct((nb, n), F32),
        compiler_params=_params(nbytes, 1),
        name="ada",
    )(c, w_ada, b_ada.reshape(1, n))


def _inproj_kernel(x_ref, sh_ref, sc_ref, wp_ref, wif_ref, u_ref, p_ref, ifg_ref):
    nb, lt, d = x_ref.shape
    u = _layer_norm(x_ref[...]) * (1.0 + sc_ref[...]) + sh_ref[...]
    ub = u.reshape(nb * lt, d).astype(BF16)
    u_ref[...] = ub
    p_ref[...] = _dot(ub, wp_ref[...])
    ifg_ref[...] = _dot(ub, wif_ref[...])


def _inproj(x, mod4, w_p, w_if, nb, lt):
    b, l, d = x.shape
    pw = w_p.shape[1]
    tm = nb * lt
    nt = l // lt
    rows = lambda bi, ti: (bi * nt + ti, 0)
    nbytes = (2 * tm * d * 4 + 2 * tm * d * 2 + 2 * tm * pw * 4 + 2 * d * pw * 2
              + 3 * tm * d * 4 + tm * pw * 4)
    return pl.pallas_call(
        _inproj_kernel,
        grid=(b // nb, nt),
        in_specs=[pl.BlockSpec((nb, lt, d), lambda bi, ti: (bi, ti, 0)),
                  pl.BlockSpec((nb, None, 1, d), lambda bi, ti: (bi, 0, 0, 0)),
                  pl.BlockSpec((nb, None, 1, d), lambda bi, ti: (bi, 1, 0, 0)),
                  pl.BlockSpec((d, pw), lambda bi, ti: (0, 0)),
                  pl.BlockSpec((d, LANES), lambda bi, ti: (0, 0))],
        out_specs=[pl.BlockSpec((tm, d), rows),
                   pl.BlockSpec((tm, pw), rows),
                   pl.BlockSpec((tm, LANES), rows)],
        out_shape=[jax.ShapeDtypeStruct((b * l, d), BF16),
                   jax.ShapeDtypeStruct((b * l, pw), F32),
                   jax.ShapeDtypeStruct((b * l, LANES), F32)],
        compiler_params=_params(nbytes, 2),
        name="inproj",
    )(x, mod4, mod4, w_p, w_if)


def _matmul_kernel(a_ref, b_ref, o_ref, *w_scr):
    if w_scr:
        @pl.when(pl.program_id(1) == 0)
        def _():
            w_scr[0][...] = b_ref[...].astype(BF16)
        w = w_scr[0][...]
    else:
        w = b_ref[...]
    o_ref[...] = _dot(a_ref[...], w).astype(o_ref.dtype)


def _matmul(a, b, col0, n, tm, tn):
    m, k = a.shape
    j0 = col0 // tn
    wsize = b.dtype.itemsize
    scratch = [pltpu.VMEM((k, tn), BF16)] if b.dtype == F32 else []
    nbytes = 2 * tm * k * 2 + (2 * wsize + 2) * k * tn + 2 * tm * tn * 2 + tm * tn * 4
    return pl.pallas_call(
        _matmul_kernel,
        grid=(n // tn, m // tm),
        in_specs=[pl.BlockSpec((tm, k), lambda j, i: (i, 0)),
                  pl.BlockSpec((k, tn), lambda j, i: (0, j + j0))],
        out_specs=pl.BlockSpec((tm, tn), lambda j, i: (i, j)),
        out_shape=jax.ShapeDtypeStruct((m, n), BF16),
        scratch_shapes=scratch,
        compiler_params=_params(nbytes, 2),
        name="bigproj",
    )(a, b)


def _mlstm_kernel(*refs, nh, hd, zero_init):
    if zero_init:
        (q_ref, k_ref, v_ref, o_ref, ifg_ref, bias_ref, gnw_ref,
         h_ref, c_out, n_out, m_out, c_scr, m_scr, d_scr, f_scr, fb_scr) = refs
    else:
        (q_ref, k_ref, v_ref, o_ref, ifg_ref, bias_ref, gnw_ref, c0_ref, n0_ref, m0_ref,
         h_ref, c_out, n_out, m_out, c_scr, m_scr, d_scr, f_scr, fb_scr) = refs
    lc = q_ref.shape[0]
    ci = pl.program_id(1)

    @pl.when(ci == 0)
    def _():
        if zero_init:
            c_scr[...] = jnp.zeros_like(c_scr)
            m_scr[...] = jnp.zeros_like(m_scr)
        else:
            c_scr[:, :, 0:hd] = c0_ref[0]
            for h in range(nh):
                c_scr[h, :, hd:hd + LANES] = jnp.transpose(jnp.broadcast_to(n0_ref[0, h], (LANES, hd)))
            m_scr[...] = m0_ref[0]

    gates = ifg_ref[...] + bias_ref[...]
    lf = _log_sigmoid(gates)
    row_t = lax.broadcasted_iota(jnp.int32, (lc, lc), 0)
    col_s = lax.broadcasted_iota(jnp.int32, (lc, lc), 1)
    causal = col_s <= row_t
    tril = jnp.where(causal, 1.0, 0.0).astype(BF16)
    b_all = sum(_dot(tril, piece) for piece in _split3(lf))
    b_col = pltpu.roll(b_all, LANES - nh, 1)
    g_col = gates - b_col
    g_row = jnp.transpose(g_col)
    scale = hd ** -0.5
    scale_is_pow2 = (hd & (hd - 1)) == 0 and (hd.bit_length() - 1) % 2 == 0
    ones_v = jnp.ones((lc, LANES), BF16)
    reps = hd // LANES
    wide = lambda col: jnp.concatenate([col] * reps, axis=1)

    m_heads, decays = [], []
    for h in range(nh):
        m_prev = m_scr[h]
        x = jnp.where(causal, g_row[h:h + 1, :], -jnp.inf)
        m_run = jnp.maximum(jnp.max(x, axis=1, keepdims=True), m_prev)
        m_run_b = jnp.broadcast_to(m_run, (lc, LANES))
        if lc % LANES == 0:
            d_scr[h] = jnp.exp(x - jnp.concatenate([m_run_b] * (lc // LANES), axis=1))
        else:
            d_scr[h] = jnp.exp(x - m_run)
        m_last = m_run[lc - 1:lc, :]
        m_row = jnp.broadcast_to(b_col[:, h:h + 1], (lc, LANES)) + m_run_b
        f_scr[h] = jnp.exp(-m_row)
        fb_scr[h, 0] = jnp.exp(m_prev - m_run_b)
        fb_scr[h, 1] = jnp.exp(jnp.broadcast_to(g_col[:, h:h + 1], (lc, LANES)) - m_last)
        decays.append(jnp.exp(m_prev - m_last))
        m_heads.append(b_col[lc - 1:lc, h:h + 1] + m_last)

    h_heads, c_heads = [], []
    for h in range(nh):
        sl = slice(h * hd, (h + 1) * hd)
        q = q_ref[:, sl]
        if scale_is_pow2:
            k = k_ref[:, sl] * scale
        else:
            k = (k_ref[:, sl].astype(F32) * scale).astype(BF16)
        v_aug = jnp.concatenate([v_ref[:, sl], ones_v], axis=1)
        c_prev = c_scr[h]

        qk = lax.dot_general(q, k, (((1,), (1,)), ((), ())), preferred_element_type=F32)
        s = (qk * d_scr[h]).astype(BF16)
        qw = (q.astype(F32) * wide(fb_scr[h, 0])).astype(BF16)
        tot = _dot(jnp.concatenate([qw, s], axis=1),
                   jnp.concatenate([c_prev.astype(BF16), v_aug], axis=0))
        inv = 1.0 / jnp.maximum(jnp.abs(tot[:, hd:]), f_scr[h])
        hg = tot[:, :hd] * wide(inv) * jax.nn.sigmoid(o_ref[:, sl].astype(F32))
        h_heads.append((_layer_norm(hg) * gnw_ref[:, sl]).astype(BF16))

        kw = (k.astype(F32) * wide(fb_scr[h, 1])).astype(BF16)
        c_heads.append(decays[h] * c_prev + lax.dot_general(
            kw, v_aug, (((0,), (0,)), ((), ())), preferred_element_type=F32))

    h_ref[...] = jnp.concatenate(h_heads, axis=1)
    c_scr[...] = jnp.stack(c_heads, axis=0)
    m_scr[...] = jnp.stack(m_heads, axis=0)

    @pl.when(ci == pl.num_programs(1) - 1)
    def _():
        c_out[0] = c_scr[:, :, 0:hd]
        n_out[0] = jnp.stack([jnp.transpose(c_scr[h, :, hd:hd + LANES])[0:1, :] for h in range(nh)], axis=0)
        m_out[0] = m_scr[...]


def _mlstm(qkvo, ifg, gate_bias, gn_w, state, b, l, lc, nh):
    mw = gn_w.shape[1]
    hd = mw // nh
    assert hd % LANES == 0
    nc = l // lc
    rows = lambda bi, ci: (bi * nc + ci, 0)
    col = lambda j: (lambda bi, ci: (bi * nc + ci, j))
    st4 = lambda bi, ci: (bi, 0, 0, 0)
    zero_init = state is None
    in_specs = [pl.BlockSpec((lc, mw), col(0)), pl.BlockSpec((lc, mw), col(1)),
                pl.BlockSpec((lc, mw), col(2)), pl.BlockSpec((lc, mw), col(3)),
                pl.BlockSpec((lc, LANES), rows),
                pl.BlockSpec((1, LANES), lambda bi, ci: (0, 0)),
                pl.BlockSpec((1, mw), lambda bi, ci: (0, 0))]
    args = [qkvo, qkvo, qkvo, qkvo, ifg, gate_bias, gn_w]
    if not zero_init:
        c0, n0, m0 = state
        in_specs += [pl.BlockSpec((1, nh, hd, hd), st4), pl.BlockSpec((1, nh, 1, hd), st4),
                     pl.BlockSpec((1, nh, 1, 1), st4)]
        args += [c0, n0.reshape(b, nh, 1, hd), m0.reshape(b, nh, 1, 1)]
    nbytes = (2 * 5 * lc * mw * 2 + 5 * nh * hd * (hd + LANES) * 4 + nh * lc * (lc + 3 * LANES) * 4
              + 16 * lc * max(lc, hd + LANES) * 4)
    h, c, n, m = pl.pallas_call(
        functools.partial(_mlstm_kernel, nh=nh, hd=hd, zero_init=zero_init),
        grid=(b, nc),
        in_specs=in_specs,
        out_specs=[pl.BlockSpec((lc, mw), rows), pl.BlockSpec((1, nh, hd, hd), st4),
                   pl.BlockSpec((1, nh, 1, hd), st4), pl.BlockSpec((1, nh, 1, 1), st4)],
        out_shape=[jax.ShapeDtypeStruct((b * l, mw), BF16),
                   jax.ShapeDtypeStruct((b, nh, hd, hd), F32),
                   jax.ShapeDtypeStruct((b, nh, 1, hd), F32),
                   jax.ShapeDtypeStruct((b, nh, 1, 1), F32)],
        scratch_shapes=[pltpu.VMEM((nh, hd, hd + LANES), F32), pltpu.VMEM((nh, 1, 1), F32),
                        pltpu.VMEM((nh, lc, lc), F32), pltpu.VMEM((nh, lc, LANES), F32),
                        pltpu.VMEM((nh, 2, lc, LANES), F32)],
        compiler_params=_params(nbytes, 2),
        name="mlstm",
    )(*args)
    return h, c, n.reshape(b, nh, hd), m.reshape(b, nh)


def _mix_kernel(*refs, start_pos, alpha, has_prefix):
    if has_prefix:
        (p_ref, pre_ref, bo_ref, ga_ref, gb_ref, x_ref, g1_ref, sh2_ref, sc2_ref, wpool_ref, psc_ref,
         wpa_ref, wpb_ref, wout_ref, l1g_ref, l1b_ref, x1_ref, u2_ref, ext_scr) = refs
    else:
        (p_ref, bo_ref, ga_ref, gb_ref, x_ref, g1_ref, sh2_ref, sc2_ref, wpool_ref, psc_ref,
         wpa_ref, wpb_ref, wout_ref, l1g_ref, l1b_ref, x1_ref, u2_ref, ext_scr) = refs
    nb, lt, pw = p_ref.shape
    d = x_ref.shape[2]
    grp = pw // len(POOL_WINDOWS)
    ti = pl.program_id(1)

    @pl.when(ti == 0)
    def _():
        ext_scr[:, 0:POOL_HALO, :] = jnp.zeros((nb, POOL_HALO, pw), F32)
        if has_prefix:
            ext_scr[:, POOL_HALO - POOL_STATE:POOL_HALO, :] = pre_ref[...]

    @pl.when(ti > 0)
    def _():
        ext_scr[:, 0:POOL_HALO, :] = ext_scr[:, lt:lt + POOL_HALO, :]

    ext_scr[:, POOL_HALO:POOL_HALO + lt, :] = p_ref[...]

    pos = start_pos + ti * lt + lax.broadcasted_iota(jnp.int32, (lt, LANES), 0)
    branch = []
    for g, w in enumerate(POOL_WINDOWS):
        cs = slice(g * grp, (g + 1) * grp)
        inv = 1.0 / jnp.minimum(pos + 1, w).astype(F32)
        inv = jnp.concatenate([inv] * (grp // LANES), axis=1)
        ys = []
        for bi in range(nb):
            ext = ext_scr[bi, :, cs]
            acc = ext
            shift = 1
            while shift < w:
                acc = acc + pltpu.roll(acc, shift, 0)
                shift *= 2
            ys.append(acc[POOL_HALO:] * inv - ext[POOL_HALO:])
        y = jnp.concatenate(ys, axis=0).astype(BF16)
        branch.append((_dot(y, wpool_ref[g]) * psc_ref[:, cs]).astype(BF16))
    a_out = jnp.concatenate(branch, axis=1)

    pa = _dot(a_out, wpa_ref[...])
    pb = _dot(bo_ref[...], wpb_ref[...])
    merged = (jax.nn.sigmoid(ga_ref[...].astype(F32)) * pa
              + jax.nn.sigmoid(gb_ref[...].astype(F32)) * pb).astype(BF16)
    t = _dot(merged, wout_ref[...]).reshape(nb, lt, d)
    x1 = _layer_norm(alpha * x_ref[...] + g1_ref[...] * t) * l1g_ref[...] + l1b_ref[...]
    x1_ref[...] = x1
    u2 = _layer_norm(x1) * (1.0 + sc2_ref[...]) + sh2_ref[...]
    u2_ref[...] = u2.reshape(nb * lt, d).astype(BF16)


def _mix(p, prefix, b_out, gg, x, mod4, w_pool, pool_scale, w_pa, w_pb, w_out, ln1_g, ln1_b,
         nb, lt, start_pos, alpha):
    b, l, d = x.shape
    pw = p.shape[1]
    mw = b_out.shape[1]
    assert (pw // len(POOL_WINDOWS)) % LANES == 0
    tm = nb * lt
    nt = l // lt
    rows = lambda bi, ti: (bi * nt + ti, 0)
    seq3 = lambda bi, ti: (bi, ti, 0)
    const2 = lambda bi, ti: (0, 0)
    modk = lambda kk: (lambda bi, ti: (bi, kk, 0, 0))
    has_prefix = prefix is not None
    single = pl.Buffered(1)
    in_specs = [pl.BlockSpec((nb, lt, pw), seq3)]
    args = [p.reshape(b, l, pw)]
    if has_prefix:
        in_specs.append(pl.BlockSpec((nb, POOL_STATE, pw), lambda bi, ti: (bi, 0, 0)))
        args.append(prefix)
    in_specs += [pl.BlockSpec((tm, mw), rows),
                 pl.BlockSpec((tm, d), rows),
                 pl.BlockSpec((tm, d), lambda bi, ti: (bi * nt + ti, 1)),
                 pl.BlockSpec((nb, lt, d), seq3),
                 pl.BlockSpec((nb, None, 1, d), modk(2)),
                 pl.BlockSpec((nb, None, 1, d), modk(3)),
                 pl.BlockSpec((nb, None, 1, d), modk(4)),
                 pl.BlockSpec(w_pool.shape, lambda bi, ti: (0, 0, 0), pipeline_mode=single),
                 pl.BlockSpec((1, pw), const2),
                 pl.BlockSpec(w_pa.shape, const2, pipeline_mode=single),
                 pl.BlockSpec(w_pb.shape, const2, pipeline_mode=single),
                 pl.BlockSpec(w_out.shape, const2, pipeline_mode=single),
                 pl.BlockSpec((1, d), const2),
                 pl.BlockSpec((1, d), const2)]
    args += [b_out, gg, gg, x, mod4, mod4, mod4, w_pool, pool_scale, w_pa, w_pb, w_out, ln1_g, ln1_b]
    wbytes = 2 * (w_pool.size + w_pa.size + w_pb.size + w_out.size)
    nbytes = (wbytes + 2 * tm * (pw * 4 + mw * 2 + 2 * d * 2 + d * 4) + 2 * tm * (d * 4 + d * 2)
              + nb * (POOL_HALO + lt) * pw * 4 + 6 * tm * d * 4)
    x1, u2 = pl.pallas_call(
        functools.partial(_mix_kernel, start_pos=start_pos, alpha=alpha, has_prefix=has_prefix),
        grid=(b // nb, nt),
        in_specs=in_specs,
        out_specs=[pl.BlockSpec((nb, lt, d), seq3), pl.BlockSpec((tm, d), rows)],
        out_shape=[jax.ShapeDtypeStruct((b, l, d), F32), jax.ShapeDtypeStruct((b * l, d), BF16)],
        scratch_shapes=[pltpu.VMEM((nb, POOL_HALO + lt, pw), F32)],
        compiler_params=_params(nbytes, 2),
        name="mix",
    )(*args)
    return x1, u2


def _ffn_up_kernel(u_ref, wg_ref, wu_ref, h_ref, wg_scr, wu_scr):
    @pl.when(pl.program_id(1) == 0)
    def _():
        wg_scr[...] = wg_ref[...].astype(BF16)
        wu_scr[...] = wu_ref[...].astype(BF16)

    u = u_ref[...]
    hg = _dot(u, wg_scr[...])
    hu = _dot(u, wu_scr[...])
    h_ref[...] = (hg * jax.nn.sigmoid(hg) * hu).astype(BF16)


def _ffn_up(u2, w_gate, w_up, tm, tn):
    m, d = u2.shape
    ff = w_gate.shape[1]
    nbytes = 2 * tm * d * 2 + 2 * (2 * 4 + 2) * d * tn + 2 * tm * tn * 2 + 4 * tm * tn * 4
    return pl.pallas_call(
        _ffn_up_kernel,
        grid=(ff // tn, m // tm),
        in_specs=[pl.BlockSpec((tm, d), lambda j, i: (i, 0)),
                  pl.BlockSpec((d, tn), lambda j, i: (0, j)),
                  pl.BlockSpec((d, tn), lambda j, i: (0, j))],
        out_specs=pl.BlockSpec((tm, tn), lambda j, i: (i, j)),
        out_shape=jax.ShapeDtypeStruct((m, ff), BF16),
        scratch_shapes=[pltpu.VMEM((d, tn), BF16), pltpu.VMEM((d, tn), BF16)],
        compiler_params=_params(nbytes, 2),
        name="ffn_up",
    )(u2, w_gate, w_up)


def _ffn_down_kernel(h_ref, wd_ref, x1_ref, g2_ref, l2g_ref, l2b_ref, y_ref, *, alpha):
    nb, lt, d = x1_ref.shape
    f = _dot(h_ref[...], wd_ref[...]).reshape(nb, lt, d)
    z = alpha * x1_ref[...] + g2_ref[...] * f
    y_ref[...] = _layer_norm(z) * l2g_ref[...] + l2b_ref[...]


def _ffn_down(hidden, x1, mod4, w_down, ln2_g, ln2_b, nb, lt, alpha):
    b, l, d = x1.shape
    ff = w_down.shape[0]
    tm = nb * lt
    nt = l // lt
    const2 = lambda bi, ti: (0, 0)
    seq3 = lambda bi, ti: (bi, ti, 0)
    nbytes = 2 * tm * ff * 2 + ff * d * 2 + 4 * tm * d * 4 + 4 * tm * d * 4
    return pl.pallas_call(
        functools.partial(_ffn_down_kernel, alpha=alpha),
        grid=(b // nb, nt),
        in_specs=[pl.BlockSpec((tm, ff), lambda bi, ti: (bi * nt + ti, 0)),
                  pl.BlockSpec((ff, d), const2, pipeline_mode=pl.Buffered(1)),
                  pl.BlockSpec((nb, lt, d), seq3),
                  pl.BlockSpec((nb, None, 1, d), lambda bi, ti: (bi, 5, 0, 0)),
                  pl.BlockSpec((1, d), const2),
                  pl.BlockSpec((1, d), const2)],
        out_specs=pl.BlockSpec((nb, lt, d), seq3),
        out_shape=jax.ShapeDtypeStruct((b, l, d), F32),
        compiler_params=_params(nbytes, 2),
        name="ffn_down",
    )(hidden, w_down, x1, mod4, ln2_g, ln2_b)


def _tile(total, want):
    if total <= want:
        return total
    t = want
    while total % t or t % 8:
        t -= 1
    return t


def _stream(x, mod, state, start_pos, w, alpha):
    b, l, d = x.shape
    nh = w["nh"]
    pw = w["pool_scale"].shape[1]
    mw = w["gn_w"].shape[1]
    mod4 = mod.reshape(b, 6, 1, d)

    def tiling(rows_want):
        lt = _tile(l, rows_want)
        nb = _tile(b, max(1, rows_want // lt)) if lt == l else 1
        return nb, lt

    nb, lt = tiling(512)
    u, p, ifg = _inproj(x, mod4, w["w_p"], w["w_if"], nb, lt)
    tm = _tile(b * l, 1024)
    qkvo = _matmul(u, w["w_in"], pw, 4 * mw, tm, 1024)
    gg = _matmul(u, w["w_gg"], 0, 2 * d, tm, 1024)

    lc = _tile(l, 256)
    mstate = None if state is None else state[1:]
    b_out, c_new, n_new, m_new = _mlstm(qkvo, ifg, w["gate_bias"], w["gn_w"], mstate, b, l, lc, nh)

    nb, lt = tiling(256)
    prefix = None if state is None else state[0]
    x1, u2 = _mix(p, prefix, b_out, gg, x, mod4, w["w_pool"], w["pool_scale"], w["w_pa"], w["w_pb"],
                  w["w_out"], w["ln1_g"], w["ln1_b"], nb, lt, start_pos, alpha)

    ff = w["w_gate"].shape[1]
    hidden = _ffn_up(u2, w["w_gate"], w["w_up"], tm, 512 if ff % 512 == 0 else ff)
    y = _ffn_down(hidden, x1, mod4, w["w_down"], w["ln2_g"], w["ln2_b"], nb, lt, alpha)
    pool_state = p.reshape(b, l, pw)[:, l - POOL_STATE:, :]
    return y, pool_state, c_new, n_new, m_new


def kernel(x_prompt, x_sample, c_prompt, c_sample, state_pool, state_mlstm_C, state_mlstm_n, state_mlstm_m, w_ada, b_ada, w_in, b_i, b_f, w_pool, pool_scale, gn_w, w_pa, w_pb, w_out, ln1_g, ln1_b, w_gate, w_up, w_down, ln2_g, ln2_b):
    depth = w_ada.shape[0]
    alpha = (2 * depth) ** 0.25
    d = x_prompt.shape[2]
    nh = b_i.shape[1]
    pw = pool_scale.shape[1]
    mw = gn_w.shape[1]
    nbp = x_prompt.shape[0]
    assert pw % 1024 == 0 and mw % 1024 == 0 and d % 1024 == 0
    y_p, y_s = x_prompt, x_sample
    outs_p, outs_s = [], []
    for li in range(depth):
        wl = w_in[li]
        gates0 = pw + 4 * mw
        w = {
            "nh": nh,
            "w_in": wl,
            "w_p": wl[:, :pw].astype(BF16),
            "w_if": jnp.pad(wl[:, gates0:gates0 + 2 * nh], ((0, 0), (0, LANES - 2 * nh))).astype(BF16),
            "w_gg": wl[:, gates0 + 2 * nh:].astype(BF16),
            "gate_bias": jnp.pad(jnp.concatenate([b_i[li], b_f[li]]), (0, LANES - 2 * nh)).reshape(1, LANES),
            "gn_w": gn_w[li].reshape(1, mw),
            "w_pool": w_pool[li].astype(BF16),
            "pool_scale": pool_scale[li].reshape(1, pw),
            "w_pa": w_pa[li].astype(BF16),
            "w_pb": w_pb[li].astype(BF16),
            "w_out": w_out[li].astype(BF16),
            "ln1_g": ln1_g[li].reshape(1, d),
            "ln1_b": ln1_b[li].reshape(1, d),
            "w_gate": w_gate[li],
            "w_up": w_up[li],
            "w_down": w_down[li].astype(BF16),
            "ln2_g": ln2_g[li].reshape(1, d),
            "ln2_b": ln2_b[li].reshape(1, d),
        }
        mod = _ada(jnp.concatenate([c_prompt, c_sample], axis=0), w_ada[li], b_ada[li])
        res_p = _stream(y_p, mod[:nbp], None, 0, w, alpha)
        res_s = _stream(y_s, mod[nbp:], (state_pool[li], state_mlstm_C[li], state_mlstm_n[li], state_mlstm_m[li]),
                        PAST_LEN, w, alpha)
        y_p, y_s = res_p[0], res_s[0]
        outs_p.append(res_p[1:])
        outs_s.append(res_s[1:])
    stack = lambda outs, k: jnp.stack([o[k] for o in outs], axis=0)
    return (y_p, y_s,
            stack(outs_p, 0), stack(outs_p, 1), stack(outs_p, 2), stack(outs_p, 3),
            stack(outs_s, 0), stack(outs_s, 1), stack(outs_s, 2), stack(outs_s, 3))
```

```python
import functools

import jax
import jax.numpy as jnp
from jax import lax
from jax.experimental import pallas as pl
from jax.experimental.pallas import tpu as pltpu

F32 = jnp.float32
BF16 = jnp.bfloat16

LN_EPS = 1e-5
POOL_WINDOWS = (2, 4, 8, 16)
POOL_STATE = max(POOL_WINDOWS) - 1
POOL_HALO = 16
PAST_LEN = 1024
LANES = 128
VMEM_CAP_BYTES = 60 * 1024 * 1024


def _vmem_limit(nbytes):
    return int(min(VMEM_CAP_BYTES, max(32 * 1024 * 1024, nbytes * 5 // 4)))


def _params(nbytes, ndims):
    return pltpu.CompilerParams(dimension_semantics=("arbitrary",) * ndims,
                                vmem_limit_bytes=_vmem_limit(nbytes))


def _layer_norm(x):
    mu = jnp.mean(x, axis=-1, keepdims=True)
    xc = x - mu
    var = jnp.mean(xc * xc, axis=-1, keepdims=True)
    return xc * lax.rsqrt(var + LN_EPS)


def _dot(a, b):
    return jnp.dot(a, b, preferred_element_type=F32)


def _dot_nt(a, bt):
    return lax.dot_general(a, bt, (((1,), (1,)), ((), ())), preferred_element_type=F32)


def _log_sigmoid(x):
    return jnp.minimum(x, 0.0) - jnp.log1p(jnp.exp(-jnp.abs(x)))


def _split3(x):
    hi = x.astype(BF16)
    r1 = x - hi.astype(F32)
    mid = r1.astype(BF16)
    lo = (r1 - mid.astype(F32)).astype(BF16)
    return hi, mid, lo


def _ada_kernel(c_ref, w_ref, b_ref, o_ref):
    c = c_ref[...]
    a = (c * jax.nn.sigmoid(c)).astype(BF16)
    o_ref[...] = _dot(a, w_ref[...].astype(BF16)) + b_ref[...]


def _ada(c, w_ada, b_ada):
    nb, d = c.shape
    n = w_ada.shape[1]
    tn = 1024 if n % 1024 == 0 else n
    nbytes = 2 * (d * tn * 4) + d * tn * 2 + 4 * nb * (d + tn) * 4
    return pl.pallas_call(
        _ada_kernel,
        grid=(n // tn,),
        in_specs=[pl.BlockSpec((nb, d), lambda j: (0, 0)),
                  pl.BlockSpec((d, tn), lambda j: (0, j)),
                  pl.BlockSpec((1, tn), lambda j: (0, j))],
        out_specs=pl.BlockSpec((nb, tn), lambda j: (0, j)),
        out_shape=jax.ShapeDtypeStruct((nb, n), F32),
        compiler_params=_params(nbytes, 1),
        name="ada",
    )(c, w_ada, b_ada.reshape(1, n))


def _inproj_kernel(x_ref, sh_ref, sc_ref, wp_ref, wif_ref, u_ref, p_ref, ifg_ref):
    nb, lt, d = x_ref.shape
    u = _layer_norm(x_ref[...]) * (1.0 + sc_ref[...]) + sh_ref[...]
    ub = u.reshape(nb * lt, d).astype(BF16)
    u_ref[...] = ub
    p_ref[...] = _dot_nt(ub, wp_ref[...])
    ifg_ref[...] = _dot_nt(ub, wif_ref[...])


def _inproj(x, mod4, w_p, w_if, nb, lt):
    b, l, d = x.shape
    pw = w_p.shape[0]
    tm = nb * lt
    nt = l // lt
    rows = lambda bi, ti: (bi * nt + ti, 0)
    nbytes = (2 * tm * d * 4 + 2 * tm * d * 2 + 2 * tm * pw * 4 + 2 * d * pw * 2
              + 3 * tm * d * 4 + tm * pw * 4)
    return pl.pallas_call(
        _inproj_kernel,
        grid=(b // nb, nt),
        in_specs=[pl.BlockSpec((nb, lt, d), lambda bi, ti: (bi, ti, 0)),
                  pl.BlockSpec((nb, None, 1, d), lambda bi, ti: (bi, 0, 0, 0)),
                  pl.BlockSpec((nb, None, 1, d), lambda bi, ti: (bi, 1, 0, 0)),
                  pl.BlockSpec((pw, d), lambda bi, ti: (0, 0)),
                  pl.BlockSpec((LANES, d), lambda bi, ti: (0, 0))],
        out_specs=[pl.BlockSpec((tm, d), rows),
                   pl.BlockSpec((tm, pw), rows),
                   pl.BlockSpec((tm, LANES), rows)],
        out_shape=[jax.ShapeDtypeStruct((b * l, d), BF16),
                   jax.ShapeDtypeStruct((b * l, pw), F32),
                   jax.ShapeDtypeStruct((b * l, LANES), F32)],
        compiler_params=_params(nbytes, 2),
        name="inproj",
    )(x, mod4, mod4, w_p, w_if)


def _matmul_kernel(a_ref, b_ref, o_ref, *w_scr):
    if w_scr:
        @pl.when(pl.program_id(1) == 0)
        def _():
            w_scr[0][...] = b_ref[...].astype(BF16)
        w = w_scr[0][...]
    else:
        w = b_ref[...]
    o_ref[...] = _dot_nt(a_ref[...], w).astype(o_ref.dtype)


def _matmul(a, b, row0, n, tm, tn):
    m, k = a.shape
    j0 = row0 // tn
    wsize = b.dtype.itemsize
    scratch = [pltpu.VMEM((tn, k), BF16)] if b.dtype == F32 else []
    nbytes = 2 * tm * k * 2 + (2 * wsize + 2) * k * tn + 2 * tm * tn * 2 + tm * tn * 4
    return pl.pallas_call(
        _matmul_kernel,
        grid=(n // tn, m // tm),
        in_specs=[pl.BlockSpec((tm, k), lambda j, i: (i, 0)),
                  pl.BlockSpec((tn, k), lambda j, i: (j + j0, 0))],
        out_specs=pl.BlockSpec((tm, tn), lambda j, i: (i, j)),
        out_shape=jax.ShapeDtypeStruct((m, n), BF16),
        scratch_shapes=scratch,
        compiler_params=_params(nbytes, 2),
        name="bigproj",
    )(a, b)


def _mlstm_kernel(*refs, nh, hd, zero_init):
    if zero_init:
        (q_ref, k_ref, v_ref, o_ref, ifg_ref, bias_ref, gnw_ref,
         h_ref, c_out, n_out, m_out, c_scr, m_scr, d_scr, f_scr, fb_scr) = refs
    else:
        (q_ref, k_ref, v_ref, o_ref, ifg_ref, bias_ref, gnw_ref, c0_ref, n0_ref, m0_ref,
         h_ref, c_out, n_out, m_out, c_scr, m_scr, d_scr, f_scr, fb_scr) = refs
    lc = q_ref.shape[0]
    ci = pl.program_id(1)

    @pl.when(ci == 0)
    def _():
        if zero_init:
            c_scr[...] = jnp.zeros_like(c_scr)
            m_scr[...] = jnp.zeros_like(m_scr)
        else:
            c_scr[:, :, 0:hd] = c0_ref[0]
            for h in range(nh):
                c_scr[h, :, hd:hd + LANES] = jnp.transpose(jnp.broadcast_to(n0_ref[0, h], (LANES, hd)))
            m_scr[...] = m0_ref[0]

    gates = ifg_ref[...] + bias_ref[...]
    lf = _log_sigmoid(gates)
    row_t = lax.broadcasted_iota(jnp.int32, (lc, lc), 0)
    col_s = lax.broadcasted_iota(jnp.int32, (lc, lc), 1)
    causal = col_s <= row_t
    tril = jnp.where(causal, 1.0, 0.0).astype(BF16)
    b_all = sum(_dot(tril, piece) for piece in _split3(lf))
    b_col = pltpu.roll(b_all, LANES - nh, 1)
    g_col = gates - b_col
    g_row = jnp.transpose(g_col)
    scale = hd ** -0.5
    scale_is_pow2 = (hd & (hd - 1)) == 0 and (hd.bit_length() - 1) % 2 == 0
    ones_v = jnp.ones((lc, LANES), BF16)
    reps = hd // LANES
    wide = lambda col: jnp.concatenate([col] * reps, axis=1)

    m_heads, decays = [], []
    for h in range(nh):
        m_prev = m_scr[h]
        x = jnp.where(causal, g_row[h:h + 1, :], -jnp.inf)
        m_run = jnp.maximum(jnp.max(x, axis=1, keepdims=True), m_prev)
        m_run_b = jnp.broadcast_to(m_run, (lc, LANES))
        if lc % LANES == 0:
            d_scr[h] = jnp.exp(x - jnp.concatenate([m_run_b] * (lc // LANES), axis=1))
        else:
            d_scr[h] = jnp.exp(x - m_run)
        m_last = m_run[lc - 1:lc, :]
        m_row = jnp.broadcast_to(b_col[:, h:h + 1], (lc, LANES)) + m_run_b
        f_scr[h] = jnp.exp(-m_row)
        fb_scr[h, 0] = jnp.exp(m_prev - m_run_b)
        fb_scr[h, 1] = jnp.exp(jnp.broadcast_to(g_col[:, h:h + 1], (lc, LANES)) - m_last)
        decays.append(jnp.exp(m_prev - m_last))
        m_heads.append(b_col[lc - 1:lc, h:h + 1] + m_last)

    h_heads, c_heads = [], []
    for h in range(nh):
        sl = slice(h * hd, (h + 1) * hd)
        q = q_ref[:, sl]
        if scale_is_pow2:
            k = k_ref[:, sl] * scale
        else:
            k = (k_ref[:, sl].astype(F32) * scale).astype(BF16)
        v_aug = jnp.concatenate([v_ref[:, sl], ones_v], axis=1)
        c_prev = c_scr[h]

        qk = lax.dot_general(q, k, (((1,), (1,)), ((), ())), preferred_element_type=F32)
        s = (qk * d_scr[h]).astype(BF16)
        qw = (q.astype(F32) * wide(fb_scr[h, 0])).astype(BF16)
        tot = _dot(jnp.concatenate([qw, s], axis=1),
                   jnp.concatenate([c_prev.astype(BF16), v_aug], axis=0))
        inv = 1.0 / jnp.maximum(jnp.abs(tot[:, hd:]), f_scr[h])
        hg = tot[:, :hd] * wide(inv) * jax.nn.sigmoid(o_ref[:, sl].astype(F32))
        h_heads.append((_layer_norm(hg) * gnw_ref[:, sl]).astype(BF16))

        kw = (k.astype(F32) * wide(fb_scr[h, 1])).astype(BF16)
        c_heads.append(decays[h] * c_prev + lax.dot_general(
            kw, v_aug, (((0,), (0,)), ((), ())), preferred_element_type=F32))

    h_ref[...] = jnp.concatenate(h_heads, axis=1)
    c_scr[...] = jnp.stack(c_heads, axis=0)
    m_scr[...] = jnp.stack(m_heads, axis=0)

    @pl.when(ci == pl.num_programs(1) - 1)
    def _():
        c_out[0] = c_scr[:, :, 0:hd]
        n_out[0] = jnp.stack([jnp.transpose(c_scr[h, :, hd:hd + LANES])[0:1, :] for h in range(nh)], axis=0)
        m_out[0] = m_scr[...]


def _mlstm(qkvo, ifg, gate_bias, gn_w, state, b, l, lc, nh):
    mw = gn_w.shape[1]
    hd = mw // nh
    assert hd % LANES == 0
    nc = l // lc
    rows = lambda bi, ci: (bi * nc + ci, 0)
    col = lambda j: (lambda bi, ci: (bi * nc + ci, j))
    st4 = lambda bi, ci: (bi, 0, 0, 0)
    zero_init = state is None
    in_specs = [pl.BlockSpec((lc, mw), col(0)), pl.BlockSpec((lc, mw), col(1)),
                pl.BlockSpec((lc, mw), col(2)), pl.BlockSpec((lc, mw), col(3)),
                pl.BlockSpec((lc, LANES), rows),
                pl.BlockSpec((1, LANES), lambda bi, ci: (0, 0)),
                pl.BlockSpec((1, mw), lambda bi, ci: (0, 0))]
    args = [qkvo, qkvo, qkvo, qkvo, ifg, gate_bias, gn_w]
    if not zero_init:
        c0, n0, m0 = state
        in_specs += [pl.BlockSpec((1, nh, hd, hd), st4), pl.BlockSpec((1, nh, 1, hd), st4),
                     pl.BlockSpec((1, nh, 1, 1), st4)]
        args += [c0, n0.reshape(b, nh, 1, hd), m0.reshape(b, nh, 1, 1)]
    nbytes = (2 * 5 * lc * mw * 2 + 5 * nh * hd * (hd + LANES) * 4 + nh * lc * (lc + 3 * LANES) * 4
              + 16 * lc * max(lc, hd + LANES) * 4)
    h, c, n, m = pl.pallas_call(
        functools.partial(_mlstm_kernel, nh=nh, hd=hd, zero_init=zero_init),
        grid=(b, nc),
        in_specs=in_specs,
        out_specs=[pl.BlockSpec((lc, mw), rows), pl.BlockSpec((1, nh, hd, hd), st4),
                   pl.BlockSpec((1, nh, 1, hd), st4), pl.BlockSpec((1, nh, 1, 1), st4)],
        out_shape=[jax.ShapeDtypeStruct((b * l, mw), BF16),
                   jax.ShapeDtypeStruct((b, nh, hd, hd), F32),
                   jax.ShapeDtypeStruct((b, nh, 1, hd), F32),
                   jax.ShapeDtypeStruct((b, nh, 1, 1), F32)],
        scratch_shapes=[pltpu.VMEM((nh, hd, hd + LANES), F32), pltpu.VMEM((nh, 1, 1), F32),
                        pltpu.VMEM((nh, lc, lc), F32), pltpu.VMEM((nh, lc, LANES), F32),
                        pltpu.VMEM((nh, 2, lc, LANES), F32)],
        compiler_params=_params(nbytes, 2),
        name="mlstm",
    )(*args)
    return h, c, n.reshape(b, nh, hd), m.reshape(b, nh)


def _mix_kernel(*refs, start_pos, alpha, has_prefix):
    if has_prefix:
        (p_ref, pre_ref, bo_ref, ga_ref, gb_ref, x_ref, g1_ref, sh2_ref, sc2_ref, wpool_ref, psc_ref,
         wpa_ref, wpb_ref, wout_ref, l1g_ref, l1b_ref, x1_ref, u2_ref, ext_scr) = refs
    else:
        (p_ref, bo_ref, ga_ref, gb_ref, x_ref, g1_ref, sh2_ref, sc2_ref, wpool_ref, psc_ref,
         wpa_ref, wpb_ref, wout_ref, l1g_ref, l1b_ref, x1_ref, u2_ref, ext_scr) = refs
    nb, lt, pw = p_ref.shape
    d = x_ref.shape[2]
    grp = pw // len(POOL_WINDOWS)
    ti = pl.program_id(1)

    @pl.when(ti == 0)
    def _():
        ext_scr[:, 0:POOL_HALO, :] = jnp.zeros((nb, POOL_HALO, pw), F32)
        if has_prefix:
            ext_scr[:, POOL_HALO - POOL_STATE:POOL_HALO, :] = pre_ref[...]

    @pl.when(ti > 0)
    def _():
        ext_scr[:, 0:POOL_HALO, :] = ext_scr[:, lt:lt + POOL_HALO, :]

    ext_scr[:, POOL_HALO:POOL_HALO + lt, :] = p_ref[...]

    pos = start_pos + ti * lt + lax.broadcasted_iota(jnp.int32, (lt, LANES), 0)
    branch = []
    for g, w in enumerate(POOL_WINDOWS):
        cs = slice(g * grp, (g + 1) * grp)
        inv = 1.0 / jnp.minimum(pos + 1, w).astype(F32)
        inv = jnp.concatenate([inv] * (grp // LANES), axis=1)
        ys = []
        for bi in range(nb):
            ext = ext_scr[bi, :, cs]
            acc = ext
            shift = 1
            while shift < w:
                acc = acc + pltpu.roll(acc, shift, 0)
                shift *= 2
            ys.append(acc[POOL_HALO:] * inv - ext[POOL_HALO:])
        y = jnp.concatenate(ys, axis=0).astype(BF16)
        branch.append((_dot(y, wpool_ref[g]) * psc_ref[:, cs]).astype(BF16))
    a_out = jnp.concatenate(branch, axis=1)

    pa = _dot(a_out, wpa_ref[...])
    pb = _dot(bo_ref[...], wpb_ref[...])
    merged = (jax.nn.sigmoid(ga_ref[...].astype(F32)) * pa
              + jax.nn.sigmoid(gb_ref[...].astype(F32)) * pb).astype(BF16)
    t = _dot(merged, wout_ref[...]).reshape(nb, lt, d)
    x1 = _layer_norm(alpha * x_ref[...] + g1_ref[...] * t) * l1g_ref[...] + l1b_ref[...]
    x1_ref[...] = x1
    u2 = _layer_norm(x1) * (1.0 + sc2_ref[...]) + sh2_ref[...]
    u2_ref[...] = u2.reshape(nb * lt, d).astype(BF16)


def _mix(p, prefix, b_out, gg, x, mod4, w_pool, pool_scale, w_pa, w_pb, w_out, ln1_g, ln1_b,
         nb, lt, start_pos, alpha):
    b, l, d = x.shape
    pw = p.shape[1]
    mw = b_out.shape[1]
    assert (pw // len(POOL_WINDOWS)) % LANES == 0
    tm = nb * lt
    nt = l // lt
    rows = lambda bi, ti: (bi * nt + ti, 0)
    seq3 = lambda bi, ti: (bi, ti, 0)
    const2 = lambda bi, ti: (0, 0)
    modk = lambda kk: (lambda bi, ti: (bi, kk, 0, 0))
    has_prefix = prefix is not None
    single = pl.Buffered(1)
    in_specs = [pl.BlockSpec((nb, lt, pw), seq3)]
    args = [p.reshape(b, l, pw)]
    if has_prefix:
        in_specs.append(pl.BlockSpec((nb, POOL_STATE, pw), lambda bi, ti: (bi, 0, 0)))
        args.append(prefix)
    in_specs += [pl.BlockSpec((tm, mw), rows),
                 pl.BlockSpec((tm, d), rows),
                 pl.BlockSpec((tm, d), lambda bi, ti: (bi * nt + ti, 1)),
                 pl.BlockSpec((nb, lt, d), seq3),
                 pl.BlockSpec((nb, None, 1, d), modk(2)),
                 pl.BlockSpec((nb, None, 1, d), modk(3)),
                 pl.BlockSpec((nb, None, 1, d), modk(4)),
                 pl.BlockSpec(w_pool.shape, lambda bi, ti: (0, 0, 0), pipeline_mode=single),
                 pl.BlockSpec((1, pw), const2),
                 pl.BlockSpec(w_pa.shape, const2, pipeline_mode=single),
                 pl.BlockSpec(w_pb.shape, const2, pipeline_mode=single),
                 pl.BlockSpec(w_out.shape, const2, pipeline_mode=single),
                 pl.BlockSpec((1, d), const2),
                 pl.BlockSpec((1, d), const2)]
    args += [b_out, gg, gg, x, mod4, mod4, mod4, w_pool, pool_scale, w_pa, w_pb, w_out, ln1_g, ln1_b]
    wbytes = 2 * (w_pool.size + w_pa.size + w_pb.size + w_out.size)
    nbytes = (wbytes + 2 * tm * (pw * 4 + mw * 2 + 2 * d * 2 + d * 4) + 2 * tm * (d * 4 + d * 2)
              + nb * (POOL_HALO + lt) * pw * 4 + 6 * tm * d * 4)
    x1, u2 = pl.pallas_call(
        functools.partial(_mix_kernel, start_pos=start_pos, alpha=alpha, has_prefix=has_prefix),
        grid=(b // nb, nt),
        in_specs=in_specs,
        out_specs=[pl.BlockSpec((nb, lt, d), seq3), pl.BlockSpec((tm, d), rows)],
        out_shape=[jax.ShapeDtypeStruct((b, l, d), F32), jax.ShapeDtypeStruct((b * l, d), BF16)],
        scratch_shapes=[pltpu.VMEM((nb, POOL_HALO + lt, pw), F32)],
        compiler_params=_params(nbytes, 2),
        name="mix",
    )(*args)
    return x1, u2


def _ffn_up_kernel(u_ref, wg_ref, wu_ref, h_ref, wg_scr, wu_scr):
    @pl.when(pl.program_id(1) == 0)
    def _():
        wg_scr[...] = wg_ref[...].astype(BF16)
        wu_scr[...] = wu_ref[...].astype(BF16)

    u = u_ref[...]
    hg = _dot(u, wg_scr[...])
    hu = _dot(u, wu_scr[...])
    h_ref[...] = (hg * jax.nn.sigmoid(hg) * hu).astype(BF16)


def _ffn_up(u2, w_gate, w_up, tm, tn):
    m, d = u2.shape
    ff = w_gate.shape[1]
    nbytes = 2 * tm * d * 2 + 2 * (2 * 4 + 2) * d * tn + 2 * tm * tn * 2 + 4 * tm * tn * 4
    return pl.pallas_call(
        _ffn_up_kernel,
        grid=(ff // tn, m // tm),
        in_specs=[pl.BlockSpec((tm, d), lambda j, i: (i, 0)),
                  pl.BlockSpec((d, tn), lambda j, i: (0, j)),
                  pl.BlockSpec((d, tn), lambda j, i: (0, j))],
        out_specs=pl.BlockSpec((tm, tn), lambda j, i: (i, j)),
        out_shape=jax.ShapeDtypeStruct((m, ff), BF16),
        scratch_shapes=[pltpu.VMEM((d, tn), BF16), pltpu.VMEM((d, tn), BF16)],
        compiler_params=_params(nbytes, 2),
        name="ffn_up",
    )(u2, w_gate, w_up)


def _ffn_down_kernel(h_ref, wd_ref, x1_ref, g2_ref, l2g_ref, l2b_ref, y_ref, *, alpha):
    nb, lt, d = x1_ref.shape
    f = _dot(h_ref[...], wd_ref[...]).reshape(nb, lt, d)
    z = alpha * x1_ref[...] + g2_ref[...] * f
    y_ref[...] = _layer_norm(z) * l2g_ref[...] + l2b_ref[...]


def _ffn_down(hidden, x1, mod4, w_down, ln2_g, ln2_b, nb, lt, alpha):
    b, l, d = x1.shape
    ff = w_down.shape[0]
    tm = nb * lt
    nt = l // lt
    const2 = lambda bi, ti: (0, 0)
    seq3 = lambda bi, ti: (bi, ti, 0)
    nbytes = 2 * tm * ff * 2 + ff * d * 2 + 4 * tm * d * 4 + 4 * tm * d * 4
    return pl.pallas_call(
        functools.partial(_ffn_down_kernel, alpha=alpha),
        grid=(b // nb, nt),
        in_specs=[pl.BlockSpec((tm, ff), lambda bi, ti: (bi * nt + ti, 0)),
                  pl.BlockSpec((ff, d), const2, pipeline_mode=pl.Buffered(1)),
                  pl.BlockSpec((nb, lt, d), seq3),
                  pl.BlockSpec((nb, None, 1, d), lambda bi, ti: (bi, 5, 0, 0)),
                  pl.BlockSpec((1, d), const2),
                  pl.BlockSpec((1, d), const2)],
        out_specs=pl.BlockSpec((nb, lt, d), seq3),
        out_shape=jax.ShapeDtypeStruct((b, l, d), F32),
        compiler_params=_params(nbytes, 2),
        name="ffn_down",
    )(hidden, w_down, x1, mod4, ln2_g, ln2_b)


def _tile(total, want):
    if total <= want:
        return total
    t = want
    while total % t or t % 8:
        t -= 1
    return t


def _stream(x, mod, state, start_pos, w, alpha):
    b, l, d = x.shape
    nh = w["nh"]
    pw = w["pool_scale"].shape[1]
    mw = w["gn_w"].shape[1]
    mod4 = mod.reshape(b, 6, 1, d)

    def tiling(rows_want):
        lt = _tile(l, rows_want)
        nb = _tile(b, max(1, rows_want // lt)) if lt == l else 1
        return nb, lt

    nb, lt = tiling(512)
    u, p, ifg = _inproj(x, mod4, w["w_p"], w["w_if"], nb, lt)
    tm = _tile(b * l, 1024)
    qkvo = _matmul(u, w["w_in_t"], pw, 4 * mw, tm, 1024)
    gg = _matmul(u, w["w_gg"], 0, 2 * d, tm, 1024)

    lc = _tile(l, 256)
    mstate = None if state is None else state[1:]
    b_out, c_new, n_new, m_new = _mlstm(qkvo, ifg, w["gate_bias"], w["gn_w"], mstate, b, l, lc, nh)

    nb, lt = tiling(256)
    prefix = None if state is None else state[0]
    x1, u2 = _mix(p, prefix, b_out, gg, x, mod4, w["w_pool"], w["pool_scale"], w["w_pa"], w["w_pb"],
                  w["w_out"], w["ln1_g"], w["ln1_b"], nb, lt, start_pos, alpha)

    ff = w["w_gate"].shape[1]
    hidden = _ffn_up(u2, w["w_gate"], w["w_up"], tm, 512 if ff % 512 == 0 else ff)
    y = _ffn_down(hidden, x1, mod4, w["w_down"], w["ln2_g"], w["ln2_b"], nb, lt, alpha)
    pool_state = p.reshape(b, l, pw)[:, l - POOL_STATE:, :]
    return y, pool_state, c_new, n_new, m_new


def kernel(x_prompt, x_sample, c_prompt, c_sample, state_pool, state_mlstm_C, state_mlstm_n, state_mlstm_m, w_ada, b_ada, w_in, b_i, b_f, w_pool, pool_scale, gn_w, w_pa, w_pb, w_out, ln1_g, ln1_b, w_gate, w_up, w_down, ln2_g, ln2_b):
    depth = w_ada.shape[0]
    alpha = (2 * depth) ** 0.25
    d = x_prompt.shape[2]
    nh = b_i.shape[1]
    pw = pool_scale.shape[1]
    mw = gn_w.shape[1]
    nbp = x_prompt.shape[0]
    assert pw % 1024 == 0 and mw % 1024 == 0 and d % 1024 == 0
    y_p, y_s = x_prompt, x_sample
    outs_p, outs_s = [], []
    for li in range(depth):
        wt = jnp.transpose(w_in[li])
        gates0 = pw + 4 * mw
        w = {
            "nh": nh,
            "w_in_t": wt,
            "w_p": wt[:pw].astype(BF16),
            "w_if": jnp.pad(wt[gates0:gates0 + 2 * nh], ((0, LANES - 2 * nh), (0, 0))).astype(BF16),
            "w_gg": wt[gates0 + 2 * nh:].astype(BF16),
            "gate_bias": jnp.pad(jnp.concatenate([b_i[li], b_f[li]]), (0, LANES - 2 * nh)).reshape(1, LANES),
            "gn_w": gn_w[li].reshape(1, mw),
            "w_pool": w_pool[li].astype(BF16),
            "pool_scale": pool_scale[li].reshape(1, pw),
            "w_pa": w_pa[li].astype(BF16),
            "w_pb": w_pb[li].astype(BF16),
            "w_out": w_out[li].astype(BF16),
            "ln1_g": ln1_g[li].reshape(1, d),
            "ln1_b": ln1_b[li].reshape(1, d),
            "w_gate": w_gate[li],
            "w_up": w_up[li],
            "w_down": w_down[li].astype(BF16),
            "ln2_g": ln2_g[li].reshape(1, d),
            "ln2_b": ln2_b[li].reshape(1, d),
        }
        mod = _ada(jnp.concatenate([c_prompt, c_sample], axis=0), w_ada[li], b_ada[li])
        res_p = _stream(y_p, mod[:nbp], None, 0, w, alpha)
        res_s = _stream(y_s, mod[nbp:], (state_pool[li], state_mlstm_C[li], state_mlstm_n[li], state_mlstm_m[li]),
                        PAST_LEN, w, alpha)
        y_p, y_s = res_p[0], res_s[0]
        outs_p.append(res_p[1:])
        outs_s.append(res_s[1:])
    stack = lambda outs, k: jnp.stack([o[k] for o in outs], axis=0)
    return (y_p, y_s,
            stack(outs_p, 0), stack(outs_p, 1), stack(outs_p, 2), stack(outs_p, 3),
            stack(outs_s, 0), stack(outs_s, 1), stack(outs_s, 2), stack(outs_s, 3))
```

```python
import functools

import jax
import jax.numpy as jnp
from jax import lax
from jax.experimental import pallas as pl
from jax.experimental.pallas import tpu as pltpu

F32 = jnp.float32
BF16 = jnp.bfloat16

LN_EPS = 1e-5
POOL_WINDOWS = (2, 4, 8, 16)
POOL_STATE = max(POOL_WINDOWS) - 1
POOL_HALO = 16
PAST_LEN = 1024
LANES = 128
VMEM_CAP_BYTES = 60 * 1024 * 1024


def _vmem_limit(nbytes):
    return int(min(VMEM_CAP_BYTES, max(32 * 1024 * 1024, nbytes * 5 // 4)))


def _params(nbytes, ndims):
    return pltpu.CompilerParams(dimension_semantics=("arbitrary",) * ndims,
                                vmem_limit_bytes=_vmem_limit(nbytes))


def _layer_norm(x):
    mu = jnp.mean(x, axis=-1, keepdims=True)
    xc = x - mu
    var = jnp.mean(xc * xc, axis=-1, keepdims=True)
    return xc * lax.rsqrt(var + LN_EPS)


def _dot(a, b):
    return jnp.dot(a, b, preferred_element_type=F32)


def _dot_nt(a, bt):
    return lax.dot_general(a, bt, (((1,), (1,)), ((), ())), preferred_element_type=F32)


def _log_sigmoid(x):
    return jnp.minimum(x, 0.0) - jnp.log1p(jnp.exp(-jnp.abs(x)))


def _split3(x):
    hi = x.astype(BF16)
    r1 = x - hi.astype(F32)
    mid = r1.astype(BF16)
    lo = (r1 - mid.astype(F32)).astype(BF16)
    return hi, mid, lo


def _ada_kernel(c_ref, w_ref, b_ref, o_ref):
    c = c_ref[...]
    a = (c * jax.nn.sigmoid(c)).astype(BF16)
    o_ref[...] = _dot(a, w_ref[...].astype(BF16)) + b_ref[...]


def _ada(c, w_ada, b_ada):
    nb, d = c.shape
    n = w_ada.shape[1]
    tn = 1024 if n % 1024 == 0 else n
    nbytes = 2 * (d * tn * 4) + d * tn * 2 + 4 * nb * (d + tn) * 4
    return pl.pallas_call(
        _ada_kernel,
        grid=(n // tn,),
        in_specs=[pl.BlockSpec((nb, d), lambda j: (0, 0)),
                  pl.BlockSpec((d, tn), lambda j: (0, j)),
                  pl.BlockSpec((1, tn), lambda j: (0, j))],
        out_specs=pl.BlockSpec((nb, tn), lambda j: (0, j)),
        out_shape=jax.ShapeDtypeStruct((nb, n), F32),
        compiler_params=_params(nbytes, 1),
        name="ada",
    )(c, w_ada, b_ada.reshape(1, n))


def _inproj_kernel(x_ref, sh_ref, sc_ref, wp_ref, wif_ref, u_ref, p_ref, ifg_ref, wp_scr):
    nb, lt, d = x_ref.shape

    @pl.when((pl.program_id(0) == 0) & (pl.program_id(1) == 0))
    def _():
        wp_scr[...] = wp_ref[...].astype(BF16)

    u = _layer_norm(x_ref[...]) * (1.0 + sc_ref[...]) + sh_ref[...]
    ub = u.reshape(nb * lt, d).astype(BF16)
    u_ref[...] = ub
    p_ref[...] = _dot_nt(ub, wp_scr[...])
    ifg_ref[...] = _dot_nt(ub, wif_ref[...])


def _inproj(x, mod4, w_rows, pw, w_if, nb, lt):
    b, l, d = x.shape
    tm = nb * lt
    nt = l // lt
    rows = lambda bi, ti: (bi * nt + ti, 0)
    nbytes = (2 * tm * d * 4 + 2 * tm * d * 2 + 2 * tm * pw * 4 + d * pw * (4 + 2)
              + 3 * tm * d * 4 + tm * pw * 4)
    return pl.pallas_call(
        _inproj_kernel,
        grid=(b // nb, nt),
        in_specs=[pl.BlockSpec((nb, lt, d), lambda bi, ti: (bi, ti, 0)),
                  pl.BlockSpec((nb, None, 1, d), lambda bi, ti: (bi, 0, 0, 0)),
                  pl.BlockSpec((nb, None, 1, d), lambda bi, ti: (bi, 1, 0, 0)),
                  pl.BlockSpec((pw, d), lambda bi, ti: (0, 0), pipeline_mode=pl.Buffered(1)),
                  pl.BlockSpec((LANES, d), lambda bi, ti: (0, 0))],
        out_specs=[pl.BlockSpec((tm, d), rows),
                   pl.BlockSpec((tm, pw), rows),
                   pl.BlockSpec((tm, LANES), rows)],
        out_shape=[jax.ShapeDtypeStruct((b * l, d), BF16),
                   jax.ShapeDtypeStruct((b * l, pw), F32),
                   jax.ShapeDtypeStruct((b * l, LANES), F32)],
        scratch_shapes=[pltpu.VMEM((pw, d), BF16)],
        compiler_params=_params(nbytes, 2),
        name="inproj",
    )(x, mod4, mod4, w_rows, w_if)


def _matmul_kernel(a_ref, b_ref, o_ref, *w_scr):
    if w_scr:
        @pl.when(pl.program_id(1) == 0)
        def _():
            w_scr[0][...] = b_ref[...].astype(BF16)
        w = w_scr[0][...]
    else:
        w = b_ref[...]
    half = a_ref.shape[0] // 2
    if half % 512 == 0:
        o_ref[:half, :] = _dot_nt(a_ref[:half, :], w).astype(o_ref.dtype)
        o_ref[half:, :] = _dot_nt(a_ref[half:, :], w).astype(o_ref.dtype)
    else:
        o_ref[...] = _dot_nt(a_ref[...], w).astype(o_ref.dtype)


def _matmul(a, b, row0, n, tm, tn):
    m, k = a.shape
    j0 = row0 // tn
    wsize = b.dtype.itemsize
    scratch = [pltpu.VMEM((tn, k), BF16)] if b.dtype == F32 else []
    nbytes = 2 * tm * k * 2 + (2 * wsize + 2) * k * tn + 2 * tm * tn * 2 + tm * tn * 4
    return pl.pallas_call(
        _matmul_kernel,
        grid=(n // tn, m // tm),
        in_specs=[pl.BlockSpec((tm, k), lambda j, i: (i, 0)),
                  pl.BlockSpec((tn, k), lambda j, i: (j + j0, 0))],
        out_specs=pl.BlockSpec((tm, tn), lambda j, i: (i, j)),
        out_shape=jax.ShapeDtypeStruct((m, n), BF16),
        scratch_shapes=scratch,
        compiler_params=_params(nbytes, 2),
        name="bigproj",
    )(a, b)


def _mlstm_kernel(*refs, nh, hd, zero_init):
    if zero_init:
        (q_ref, k_ref, v_ref, o_ref, ifg_ref, bias_ref, gnw_ref,
         h_ref, c_out, n_out, m_out, c_scr, m_scr, d_scr, f_scr, fb_scr) = refs
    else:
        (q_ref, k_ref, v_ref, o_ref, ifg_ref, bias_ref, gnw_ref, c0_ref, n0_ref, m0_ref,
         h_ref, c_out, n_out, m_out, c_scr, m_scr, d_scr, f_scr, fb_scr) = refs
    lc = q_ref.shape[0]
    ci = pl.program_id(1)

    @pl.when(ci == 0)
    def _():
        if zero_init:
            c_scr[...] = jnp.zeros_like(c_scr)
            m_scr[...] = jnp.zeros_like(m_scr)
        else:
            c_scr[:, :, 0:hd] = c0_ref[0]
            for h in range(nh):
                c_scr[h, :, hd:hd + LANES] = jnp.transpose(jnp.broadcast_to(n0_ref[0, h], (LANES, hd)))
            m_scr[...] = m0_ref[0]

    gates = ifg_ref[...] + bias_ref[...]
    lf = _log_sigmoid(gates)
    row_t = lax.broadcasted_iota(jnp.int32, (lc, lc), 0)
    col_s = lax.broadcasted_iota(jnp.int32, (lc, lc), 1)
    causal = col_s <= row_t
    tril = jnp.where(causal, 1.0, 0.0).astype(BF16)
    b_all = sum(_dot(tril, piece) for piece in _split3(lf))
    b_col = pltpu.roll(b_all, LANES - nh, 1)
    g_col = gates - b_col
    g_row = jnp.transpose(g_col)
    scale = hd ** -0.5
    scale_is_pow2 = (hd & (hd - 1)) == 0 and (hd.bit_length() - 1) % 2 == 0
    ones_v = jnp.ones((lc, LANES), BF16)
    reps = hd // LANES
    wide = lambda col: jnp.concatenate([col] * reps, axis=1)

    m_heads, decays = [], []
    for h in range(nh):
        m_prev = m_scr[h]
        x = jnp.where(causal, g_row[h:h + 1, :], -jnp.inf)
        m_run = jnp.maximum(jnp.max(x, axis=1, keepdims=True), m_prev)
        m_run_b = jnp.broadcast_to(m_run, (lc, LANES))
        if lc % LANES == 0:
            d_scr[h] = jnp.exp(x - jnp.concatenate([m_run_b] * (lc // LANES), axis=1))
        else:
            d_scr[h] = jnp.exp(x - m_run)
        m_last = m_run[lc - 1:lc, :]
        m_row = jnp.broadcast_to(b_col[:, h:h + 1], (lc, LANES)) + m_run_b
        f_scr[h] = jnp.exp(-m_row)
        fb_scr[h, 0] = jnp.exp(m_prev - m_run_b)
        fb_scr[h, 1] = jnp.exp(jnp.broadcast_to(g_col[:, h:h + 1], (lc, LANES)) - m_last)
        decays.append(jnp.exp(m_prev - m_last))
        m_heads.append(b_col[lc - 1:lc, h:h + 1] + m_last)

    h_heads, c_heads = [], []
    for h in range(nh):
        sl = slice(h * hd, (h + 1) * hd)
        q = q_ref[:, sl]
        if scale_is_pow2:
            k = k_ref[:, sl] * scale
        else:
            k = (k_ref[:, sl].astype(F32) * scale).astype(BF16)
        v_aug = jnp.concatenate([v_ref[:, sl], ones_v], axis=1)
        c_prev = c_scr[h]

        qk = lax.dot_general(q, k, (((1,), (1,)), ((), ())), preferred_element_type=F32)
        s = (qk * d_scr[h]).astype(BF16)
        qw = (q.astype(F32) * wide(fb_scr[h, 0])).astype(BF16)
        tot = _dot(jnp.concatenate([qw, s], axis=1),
                   jnp.concatenate([c_prev.astype(BF16), v_aug], axis=0))
        inv = 1.0 / jnp.maximum(jnp.abs(tot[:, hd:]), f_scr[h])
        hg = tot[:, :hd] * wide(inv) * jax.nn.sigmoid(o_ref[:, sl].astype(F32))
        h_heads.append((_layer_norm(hg) * gnw_ref[:, sl]).astype(BF16))

        kw = (k.astype(F32) * wide(fb_scr[h, 1])).astype(BF16)
        c_heads.append(decays[h] * c_prev + lax.dot_general(
            kw, v_aug, (((0,), (0,)), ((), ())), preferred_element_type=F32))

    h_ref[...] = jnp.concatenate(h_heads, axis=1)
    c_scr[...] = jnp.stack(c_heads, axis=0)
    m_scr[...] = jnp.stack(m_heads, axis=0)

    @pl.when(ci == pl.num_programs(1) - 1)
    def _():
        c_out[0] = c_scr[:, :, 0:hd]
        n_out[0] = jnp.stack([jnp.transpose(c_scr[h, :, hd:hd + LANES])[0:1, :] for h in range(nh)], axis=0)
        m_out[0] = m_scr[...]


def _mlstm(qkvo, ifg, gate_bias, gn_w, state, b, l, lc, nh):
    mw = gn_w.shape[1]
    hd = mw // nh
    assert hd % LANES == 0
    nc = l // lc
    rows = lambda bi, ci: (bi * nc + ci, 0)
    col = lambda j: (lambda bi, ci: (bi * nc + ci, j))
    st4 = lambda bi, ci: (bi, 0, 0, 0)
    zero_init = state is None
    in_specs = [pl.BlockSpec((lc, mw), col(0)), pl.BlockSpec((lc, mw), col(1)),
                pl.BlockSpec((lc, mw), col(2)), pl.BlockSpec((lc, mw), col(3)),
                pl.BlockSpec((lc, LANES), rows),
                pl.BlockSpec((1, LANES), lambda bi, ci: (0, 0)),
                pl.BlockSpec((1, mw), lambda bi, ci: (0, 0))]
    args = [qkvo, qkvo, qkvo, qkvo, ifg, gate_bias, gn_w]
    if not zero_init:
        c0, n0, m0 = state
        in_specs += [pl.BlockSpec((1, nh, hd, hd), st4), pl.BlockSpec((1, nh, 1, hd), st4),
                     pl.BlockSpec((1, nh, 1, 1), st4)]
        args += [c0, n0.reshape(b, nh, 1, hd), m0.reshape(b, nh, 1, 1)]
    nbytes = (2 * 5 * lc * mw * 2 + 5 * nh * hd * (hd + LANES) * 4 + nh * lc * (lc + 3 * LANES) * 4
              + 16 * lc * max(lc, hd + LANES) * 4)
    h, c, n, m = pl.pallas_call(
        functools.partial(_mlstm_kernel, nh=nh, hd=hd, zero_init=zero_init),
        grid=(b, nc),
        in_specs=in_specs,
        out_specs=[pl.BlockSpec((lc, mw), rows), pl.BlockSpec((1, nh, hd, hd), st4),
                   pl.BlockSpec((1, nh, 1, hd), st4), pl.BlockSpec((1, nh, 1, 1), st4)],
        out_shape=[jax.ShapeDtypeStruct((b * l, mw), BF16),
                   jax.ShapeDtypeStruct((b, nh, hd, hd), F32),
                   jax.ShapeDtypeStruct((b, nh, 1, hd), F32),
                   jax.ShapeDtypeStruct((b, nh, 1, 1), F32)],
        scratch_shapes=[pltpu.VMEM((nh, hd, hd + LANES), F32), pltpu.VMEM((nh, 1, 1), F32),
                        pltpu.VMEM((nh, lc, lc), F32), pltpu.VMEM((nh, lc, LANES), F32),
                        pltpu.VMEM((nh, 2, lc, LANES), F32)],
        compiler_params=_params(nbytes, 2),
        name="mlstm",
    )(*args)
    return h, c, n.reshape(b, nh, hd), m.reshape(b, nh)


def _mix_kernel(*refs, start_pos, alpha, has_prefix):
    if has_prefix:
        (p_ref, pre_ref, bo_ref, ga_ref, gb_ref, x_ref, g1_ref, sh2_ref, sc2_ref, wpool_ref, psc_ref,
         wpa_ref, wpb_ref, wout_ref, l1g_ref, l1b_ref, x1_ref, u2_ref, ext_scr) = refs
    else:
        (p_ref, bo_ref, ga_ref, gb_ref, x_ref, g1_ref, sh2_ref, sc2_ref, wpool_ref, psc_ref,
         wpa_ref, wpb_ref, wout_ref, l1g_ref, l1b_ref, x1_ref, u2_ref, ext_scr) = refs
    nb, lt, pw = p_ref.shape
    d = x_ref.shape[2]
    grp = pw // len(POOL_WINDOWS)
    ti = pl.program_id(1)

    @pl.when(ti == 0)
    def _():
        ext_scr[:, 0:POOL_HALO, :] = jnp.zeros((nb, POOL_HALO, pw), F32)
        if has_prefix:
            ext_scr[:, POOL_HALO - POOL_STATE:POOL_HALO, :] = pre_ref[...]

    @pl.when(ti > 0)
    def _():
        ext_scr[:, 0:POOL_HALO, :] = ext_scr[:, lt:lt + POOL_HALO, :]

    ext_scr[:, POOL_HALO:POOL_HALO + lt, :] = p_ref[...]

    pos = start_pos + ti * lt + lax.broadcasted_iota(jnp.int32, (lt, LANES), 0)
    branch = []
    for g, w in enumerate(POOL_WINDOWS):
        cs = slice(g * grp, (g + 1) * grp)
        inv = 1.0 / jnp.minimum(pos + 1, w).astype(F32)
        inv = jnp.concatenate([inv] * (grp // LANES), axis=1)
        ys = []
        for bi in range(nb):
            ext = ext_scr[bi, :, cs]
            acc = ext
            shift = 1
            while shift < w:
                acc = acc + pltpu.roll(acc, shift, 0)
                shift *= 2
            ys.append(acc[POOL_HALO:] * inv - ext[POOL_HALO:])
        y = jnp.concatenate(ys, axis=0).astype(BF16)
        branch.append((_dot(y, wpool_ref[g]) * psc_ref[:, cs]).astype(BF16))
    a_out = jnp.concatenate(branch, axis=1)

    pa = _dot(a_out, wpa_ref[...])
    pb = _dot(bo_ref[...], wpb_ref[...])
    merged = (jax.nn.sigmoid(ga_ref[...].astype(F32)) * pa
              + jax.nn.sigmoid(gb_ref[...].astype(F32)) * pb).astype(BF16)
    t = _dot(merged, wout_ref[...]).reshape(nb, lt, d)
    x1 = _layer_norm(alpha * x_ref[...] + g1_ref[...] * t) * l1g_ref[...] + l1b_ref[...]
    x1_ref[...] = x1
    u2 = _layer_norm(x1) * (1.0 + sc2_ref[...]) + sh2_ref[...]
    u2_ref[...] = u2.reshape(nb * lt, d).astype(BF16)


def _mix(p, prefix, b_out, gg, x, mod4, w_pool, pool_scale, w_pa, w_pb, w_out, ln1_g, ln1_b,
         nb, lt, start_pos, alpha):
    b, l, d = x.shape
    pw = p.shape[1]
    mw = b_out.shape[1]
    assert (pw // len(POOL_WINDOWS)) % LANES == 0
    tm = nb * lt
    nt = l // lt
    rows = lambda bi, ti: (bi * nt + ti, 0)
    seq3 = lambda bi, ti: (bi, ti, 0)
    const2 = lambda bi, ti: (0, 0)
    modk = lambda kk: (lambda bi, ti: (bi, kk, 0, 0))
    has_prefix = prefix is not None
    single = pl.Buffered(1)
    in_specs = [pl.BlockSpec((nb, lt, pw), seq3)]
    args = [p.reshape(b, l, pw)]
    if has_prefix:
        in_specs.append(pl.BlockSpec((nb, POOL_STATE, pw), lambda bi, ti: (bi, 0, 0)))
        args.append(prefix)
    in_specs += [pl.BlockSpec((tm, mw), rows),
                 pl.BlockSpec((tm, d), rows),
                 pl.BlockSpec((tm, d), lambda bi, ti: (bi * nt + ti, 1)),
                 pl.BlockSpec((nb, lt, d), seq3),
                 pl.BlockSpec((nb, None, 1, d), modk(2)),
                 pl.BlockSpec((nb, None, 1, d), modk(3)),
                 pl.BlockSpec((nb, None, 1, d), modk(4)),
                 pl.BlockSpec(w_pool.shape, lambda bi, ti: (0, 0, 0), pipeline_mode=single),
                 pl.BlockSpec((1, pw), const2),
                 pl.BlockSpec(w_pa.shape, const2, pipeline_mode=single),
                 pl.BlockSpec(w_pb.shape, const2, pipeline_mode=single),
                 pl.BlockSpec(w_out.shape, const2, pipeline_mode=single),
                 pl.BlockSpec((1, d), const2),
                 pl.BlockSpec((1, d), const2)]
    args += [b_out, gg, gg, x, mod4, mod4, mod4, w_pool, pool_scale, w_pa, w_pb, w_out, ln1_g, ln1_b]
    wbytes = 2 * (w_pool.size + w_pa.size + w_pb.size + w_out.size)
    nbytes = (wbytes + 2 * tm * (pw * 4 + mw * 2 + 2 * d * 2 + d * 4) + 2 * tm * (d * 4 + d * 2)
              + nb * (POOL_HALO + lt) * pw * 4 + 6 * tm * d * 4)
    x1, u2 = pl.pallas_call(
        functools.partial(_mix_kernel, start_pos=start_pos, alpha=alpha, has_prefix=has_prefix),
        grid=(b // nb, nt),
        in_specs=in_specs,
        out_specs=[pl.BlockSpec((nb, lt, d), seq3), pl.BlockSpec((tm, d), rows)],
        out_shape=[jax.ShapeDtypeStruct((b, l, d), F32), jax.ShapeDtypeStruct((b * l, d), BF16)],
        scratch_shapes=[pltpu.VMEM((nb, POOL_HALO + lt, pw), F32)],
        compiler_params=_params(nbytes, 2),
        name="mix",
    )(*args)
    return x1, u2


def _ffn_up_kernel(u_ref, wg_ref, wu_ref, h_ref, wg_scr, wu_scr):
    @pl.when(pl.program_id(1) == 0)
    def _():
        wg_scr[...] = wg_ref[...].astype(BF16)
        wu_scr[...] = wu_ref[...].astype(BF16)

    half = u_ref.shape[0] // 2
    parts = [slice(0, half), slice(half, 2 * half)] if half % 512 == 0 else [slice(None)]
    for rows in parts:
        u = u_ref[rows, :]
        hg = _dot(u, wg_scr[...])
        hu = _dot(u, wu_scr[...])
        h_ref[rows, :] = (hg * jax.nn.sigmoid(hg) * hu).astype(BF16)


def _ffn_up(u2, w_gate, w_up, tm, tn):
    m, d = u2.shape
    ff = w_gate.shape[1]
    nbytes = 2 * tm * d * 2 + 2 * (2 * 4 + 2) * d * tn + 2 * tm * tn * 2 + 4 * tm * tn * 4
    return pl.pallas_call(
        _ffn_up_kernel,
        grid=(ff // tn, m // tm),
        in_specs=[pl.BlockSpec((tm, d), lambda j, i: (i, 0)),
                  pl.BlockSpec((d, tn), lambda j, i: (0, j)),
                  pl.BlockSpec((d, tn), lambda j, i: (0, j))],
        out_specs=pl.BlockSpec((tm, tn), lambda j, i: (i, j)),
        out_shape=jax.ShapeDtypeStruct((m, ff), BF16),
        scratch_shapes=[pltpu.VMEM((d, tn), BF16), pltpu.VMEM((d, tn), BF16)],
        compiler_params=_params(nbytes, 2),
        name="ffn_up",
    )(u2, w_gate, w_up)


def _ffn_down_kernel(h_ref, wd_ref, x1_ref, g2_ref, l2g_ref, l2b_ref, y_ref, *, alpha):
    nb, lt, d = x1_ref.shape
    f = _dot(h_ref[...], wd_ref[...]).reshape(nb, lt, d)
    z = alpha * x1_ref[...] + g2_ref[...] * f
    y_ref[...] = _layer_norm(z) * l2g_ref[...] + l2b_ref[...]


def _ffn_down(hidden, x1, mod4, w_down, ln2_g, ln2_b, nb, lt, alpha):
    b, l, d = x1.shape
    ff = w_down.shape[0]
    tm = nb * lt
    nt = l // lt
    const2 = lambda bi, ti: (0, 0)
    seq3 = lambda bi, ti: (bi, ti, 0)
    nbytes = 2 * tm * ff * 2 + ff * d * 2 + 4 * tm * d * 4 + 4 * tm * d * 4
    return pl.pallas_call(
        functools.partial(_ffn_down_kernel, alpha=alpha),
        grid=(b // nb, nt),
        in_specs=[pl.BlockSpec((tm, ff), lambda bi, ti: (bi * nt + ti, 0)),
                  pl.BlockSpec((ff, d), const2, pipeline_mode=pl.Buffered(1)),
                  pl.BlockSpec((nb, lt, d), seq3),
                  pl.BlockSpec((nb, None, 1, d), lambda bi, ti: (bi, 5, 0, 0)),
                  pl.BlockSpec((1, d), const2),
                  pl.BlockSpec((1, d), const2)],
        out_specs=pl.BlockSpec((nb, lt, d), seq3),
        out_shape=jax.ShapeDtypeStruct((b, l, d), F32),
        compiler_params=_params(nbytes, 2),
        name="ffn_down",
    )(hidden, w_down, x1, mod4, ln2_g, ln2_b)


def _tile(total, want):
    if total <= want:
        return total
    t = want
    while total % t or t % 8:
        t -= 1
    return t


def _stream(x, mod, state, start_pos, w, alpha):
    b, l, d = x.shape
    nh = w["nh"]
    pw = w["pool_scale"].shape[1]
    mw = w["gn_w"].shape[1]
    mod4 = mod.reshape(b, 6, 1, d)

    def tiling(rows_want):
        lt = _tile(l, rows_want)
        nb = _tile(b, max(1, rows_want // lt)) if lt == l else 1
        return nb, lt

    nb, lt = tiling(512)
    u, p, ifg = _inproj(x, mod4, w["w_in_t"], pw, w["w_if"], nb, lt)
    tm = _tile(b * l, 1024)
    qkvo = _matmul(u, w["w_in_t"], pw, 4 * mw, _tile(b * l, 2048), 1024)
    gg = _matmul(u, w["w_gg"], 0, 2 * d, _tile(b * l, 2048), 1024)

    lc = _tile(l, 256)
    mstate = None if state is None else state[1:]
    b_out, c_new, n_new, m_new = _mlstm(qkvo, ifg, w["gate_bias"], w["gn_w"], mstate, b, l, lc, nh)

    nb, lt = tiling(256)
    prefix = None if state is None else state[0]
    x1, u2 = _mix(p, prefix, b_out, gg, x, mod4, w["w_pool"], w["pool_scale"], w["w_pa"], w["w_pb"],
                  w["w_out"], w["ln1_g"], w["ln1_b"], nb, lt, start_pos, alpha)

    ff = w["w_gate"].shape[1]
    hidden = _ffn_up(u2, w["w_gate"], w["w_up"], _tile(b * l, 2048), 512 if ff % 512 == 0 else ff)
    y = _ffn_down(hidden, x1, mod4, w["w_down"], w["ln2_g"], w["ln2_b"], nb, lt, alpha)
    pool_state = p.reshape(b, l, pw)[:, l - POOL_STATE:, :]
    return y, pool_state, c_new, n_new, m_new


def kernel(x_prompt, x_sample, c_prompt, c_sample, state_pool, state_mlstm_C, state_mlstm_n, state_mlstm_m, w_ada, b_ada, w_in, b_i, b_f, w_pool, pool_scale, gn_w, w_pa, w_pb, w_out, ln1_g, ln1_b, w_gate, w_up, w_down, ln2_g, ln2_b):
    depth = w_ada.shape[0]
    alpha = (2 * depth) ** 0.25
    d = x_prompt.shape[2]
    nh = b_i.shape[1]
    pw = pool_scale.shape[1]
    mw = gn_w.shape[1]
    nbp = x_prompt.shape[0]
    assert pw % 1024 == 0 and mw % 1024 == 0 and d % 1024 == 0
    y_p, y_s = x_prompt, x_sample
    outs_p, outs_s = [], []
    for li in range(depth):
        wt = jnp.transpose(w_in[li])
        gates0 = pw + 4 * mw
        w = {
            "nh": nh,
            "w_in_t": wt,
            "w_if": jnp.pad(wt[gates0:gates0 + 2 * nh], ((0, LANES - 2 * nh), (0, 0))).astype(BF16),
            "w_gg": wt[gates0 + 2 * nh:],
            "gate_bias": jnp.pad(jnp.concatenate([b_i[li], b_f[li]]), (0, LANES - 2 * nh)).reshape(1, LANES),
            "gn_w": gn_w[li].reshape(1, mw),
            "w_pool": w_pool[li].astype(BF16),
            "pool_scale": pool_scale[li].reshape(1, pw),
            "w_pa": w_pa[li].astype(BF16),
            "w_pb": w_pb[li].astype(BF16),
            "w_out": w_out[li].astype(BF16),
            "ln1_g": ln1_g[li].reshape(1, d),
            "ln1_b": ln1_b[li].reshape(1, d),
            "w_gate": w_gate[li],
            "w_up": w_up[li],
            "w_down": w_down[li].astype(BF16),
            "ln2_g": ln2_g[li].reshape(1, d),
            "ln2_b": ln2_b[li].reshape(1, d),
        }
        mod = _ada(jnp.concatenate([c_prompt, c_sample], axis=0), w_ada[li], b_ada[li])
        res_p = _stream(y_p, mod[:nbp], None, 0, w, alpha)
        res_s = _stream(y_s, mod[nbp:], (state_pool[li], state_mlstm_C[li], state_mlstm_n[li], state_mlstm_m[li]),
                        PAST_LEN, w, alpha)
        y_p, y_s = res_p[0], res_s[0]
        outs_p.append(res_p[1:])
        outs_s.append(res_s[1:])
    stack = lambda outs, k: jnp.stack([o[k] for o in outs], axis=0)
    return (y_p, y_s,
            stack(outs_p, 0), stack(outs_p, 1), stack(outs_p, 2), stack(outs_p, 3),
            stack(outs_s, 0), stack(outs_s, 1), stack(outs_s, 2), stack(outs_s, 3))
```

```python
import functools

import jax
import jax.numpy as jnp
from jax import lax
from jax.experimental import pallas as pl
from jax.experimental.pallas import tpu as pltpu

F32 = jnp.float32
BF16 = jnp.bfloat16

LN_EPS = 1e-5
POOL_WINDOWS = (2, 4, 8, 16)
POOL_STATE = max(POOL_WINDOWS) - 1
POOL_HALO = 16
PAST_LEN = 1024
LANES = 128
VMEM_CAP_BYTES = 60 * 1024 * 1024


def _vmem_limit(nbytes):
    return int(min(VMEM_CAP_BYTES, max(32 * 1024 * 1024, nbytes * 5 // 4)))


def _params(nbytes, ndims):
    return pltpu.CompilerParams(dimension_semantics=("arbitrary",) * ndims,
                                vmem_limit_bytes=_vmem_limit(nbytes))


def _layer_norm(x):
    mu = jnp.mean(x, axis=-1, keepdims=True)
    xc = x - mu
    var = jnp.mean(xc * xc, axis=-1, keepdims=True)
    return xc * lax.rsqrt(var + LN_EPS)


MIN_MATMUL_ROWS = 256


def _row_parts(nb, lt):
    if nb == 1 and lt % (2 * MIN_MATMUL_ROWS) == 0:
        h = lt // 2
        return [(slice(0, 1), slice(0, h), slice(0, h)), (slice(0, 1), slice(h, lt), slice(h, lt))]
    if nb % 2 == 0 and (nb // 2 * lt) % MIN_MATMUL_ROWS == 0:
        h = nb // 2
        return [(slice(0, h), slice(0, lt), slice(0, h * lt)), (slice(h, nb), slice(0, lt), slice(h * lt, nb * lt))]
    return [(slice(0, nb), slice(0, lt), slice(0, nb * lt))]


def _dot(a, b):
    return jnp.dot(a, b, preferred_element_type=F32)


def _dot_nt(a, bt):
    return lax.dot_general(a, bt, (((1,), (1,)), ((), ())), preferred_element_type=F32)


def _log_sigmoid(x):
    return jnp.minimum(x, 0.0) - jnp.log1p(jnp.exp(-jnp.abs(x)))


def _split3(x):
    hi = x.astype(BF16)
    r1 = x - hi.astype(F32)
    mid = r1.astype(BF16)
    lo = (r1 - mid.astype(F32)).astype(BF16)
    return hi, mid, lo


def _ada_kernel(c_ref, w_ref, b_ref, o_ref):
    c = c_ref[...]
    a = (c * jax.nn.sigmoid(c)).astype(BF16)
    o_ref[...] = _dot(a, w_ref[...].astype(BF16)) + b_ref[...]


def _ada(c, w_ada, b_ada):
    nb, d = c.shape
    n = w_ada.shape[1]
    tn = 1024 if n % 1024 == 0 else n
    nbytes = 2 * (d * tn * 4) + d * tn * 2 + 4 * nb * (d + tn) * 4
    return pl.pallas_call(
        _ada_kernel,
        grid=(n // tn,),
        in_specs=[pl.BlockSpec((nb, d), lambda j: (0, 0)),
                  pl.BlockSpec((d, tn), lambda j: (0, j)),
                  pl.BlockSpec((1, tn), lambda j: (0, j))],
        out_specs=pl.BlockSpec((nb, tn), lambda j: (0, j)),
        out_shape=jax.ShapeDtypeStruct((nb, n), F32),
        compiler_params=_params(nbytes, 1),
        name="ada",
    )(c, w_ada, b_ada.reshape(1, n))


def _inproj_kernel(x_ref, sh_ref, sc_ref, wp_ref, wif_ref, u_ref, p_ref, ifg_ref, wp_scr):
    nb, lt, d = x_ref.shape

    @pl.when((pl.program_id(0) == 0) & (pl.program_id(1) == 0))
    def _():
        wp_scr[...] = wp_ref[...].astype(BF16)

    for bs, ts, rs in _row_parts(nb, lt):
        u = _layer_norm(x_ref[bs, ts, :]) * (1.0 + sc_ref[bs, :, :]) + sh_ref[bs, :, :]
        ub = u.reshape(rs.stop - rs.start, d).astype(BF16)
        u_ref[rs, :] = ub
        p_ref[rs, :] = _dot_nt(ub, wp_scr[...])
        ifg_ref[rs, :] = _dot_nt(ub, wif_ref[...])


def _inproj(x, mod4, w_rows, pw, w_if, nb, lt):
    b, l, d = x.shape
    tm = nb * lt
    nt = l // lt
    rows = lambda bi, ti: (bi * nt + ti, 0)
    nbytes = (2 * tm * d * 4 + 2 * tm * d * 2 + 2 * tm * pw * 4 + d * pw * (4 + 2)
              + 3 * tm * d * 4 + tm * pw * 4)
    return pl.pallas_call(
        _inproj_kernel,
        grid=(b // nb, nt),
        in_specs=[pl.BlockSpec((nb, lt, d), lambda bi, ti: (bi, ti, 0)),
                  pl.BlockSpec((nb, None, 1, d), lambda bi, ti: (bi, 0, 0, 0)),
                  pl.BlockSpec((nb, None, 1, d), lambda bi, ti: (bi, 1, 0, 0)),
                  pl.BlockSpec((pw, d), lambda bi, ti: (0, 0), pipeline_mode=pl.Buffered(1)),
                  pl.BlockSpec((LANES, d), lambda bi, ti: (0, 0))],
        out_specs=[pl.BlockSpec((tm, d), rows),
                   pl.BlockSpec((tm, pw), rows),
                   pl.BlockSpec((tm, LANES), rows)],
        out_shape=[jax.ShapeDtypeStruct((b * l, d), BF16),
                   jax.ShapeDtypeStruct((b * l, pw), F32),
                   jax.ShapeDtypeStruct((b * l, LANES), F32)],
        scratch_shapes=[pltpu.VMEM((pw, d), BF16)],
        compiler_params=_params(nbytes, 2),
        name="inproj",
    )(x, mod4, mod4, w_rows, w_if)


def _matmul_kernel(a_ref, b_ref, o_ref, *w_scr):
    if w_scr:
        @pl.when(pl.program_id(1) == 0)
        def _():
            w_scr[0][...] = b_ref[...].astype(BF16)
        w = w_scr[0][...]
    else:
        w = b_ref[...]
    half = a_ref.shape[0] // 2
    if half % 512 == 0:
        o_ref[:half, :] = _dot_nt(a_ref[:half, :], w).astype(o_ref.dtype)
        o_ref[half:, :] = _dot_nt(a_ref[half:, :], w).astype(o_ref.dtype)
    else:
        o_ref[...] = _dot_nt(a_ref[...], w).astype(o_ref.dtype)


def _matmul(a, b, row0, n, tm, tn):
    m, k = a.shape
    j0 = row0 // tn
    wsize = b.dtype.itemsize
    scratch = [pltpu.VMEM((tn, k), BF16)] if b.dtype == F32 else []
    nbytes = 2 * tm * k * 2 + (2 * wsize + 2) * k * tn + 2 * tm * tn * 2 + tm * tn * 4
    return pl.pallas_call(
        _matmul_kernel,
        grid=(n // tn, m // tm),
        in_specs=[pl.BlockSpec((tm, k), lambda j, i: (i, 0)),
                  pl.BlockSpec((tn, k), lambda j, i: (j + j0, 0))],
        out_specs=pl.BlockSpec((tm, tn), lambda j, i: (i, j)),
        out_shape=jax.ShapeDtypeStruct((m, n), BF16),
        scratch_shapes=scratch,
        compiler_params=_params(nbytes, 2),
        name="bigproj",
    )(a, b)


def _mlstm_kernel(*refs, nh, hd, zero_init):
    if zero_init:
        (q_ref, k_ref, v_ref, o_ref, ifg_ref, bias_ref, gnw_ref,
         h_ref, c_out, n_out, m_out, c_scr, m_scr, d_scr, f_scr, fb_scr) = refs
    else:
        (q_ref, k_ref, v_ref, o_ref, ifg_ref, bias_ref, gnw_ref, c0_ref, n0_ref, m0_ref,
         h_ref, c_out, n_out, m_out, c_scr, m_scr, d_scr, f_scr, fb_scr) = refs
    lc = q_ref.shape[0]
    ci = pl.program_id(1)

    @pl.when(ci == 0)
    def _():
        if zero_init:
            c_scr[...] = jnp.zeros_like(c_scr)
            m_scr[...] = jnp.zeros_like(m_scr)
        else:
            c_scr[:, :, 0:hd] = c0_ref[0]
            for h in range(nh):
                c_scr[h, :, hd:hd + LANES] = jnp.transpose(jnp.broadcast_to(n0_ref[0, h], (LANES, hd)))
            m_scr[...] = m0_ref[0]

    gates = ifg_ref[...] + bias_ref[...]
    lf = _log_sigmoid(gates)
    row_t = lax.broadcasted_iota(jnp.int32, (lc, lc), 0)
    col_s = lax.broadcasted_iota(jnp.int32, (lc, lc), 1)
    causal = col_s <= row_t
    tril = jnp.where(causal, 1.0, 0.0).astype(BF16)
    b_all = sum(_dot(tril, piece) for piece in _split3(lf))
    b_col = pltpu.roll(b_all, LANES - nh, 1)
    g_col = gates - b_col
    g_row = jnp.transpose(g_col)
    scale = hd ** -0.5
    scale_is_pow2 = (hd & (hd - 1)) == 0 and (hd.bit_length() - 1) % 2 == 0
    ones_v = jnp.ones((lc, LANES), BF16)
    reps = hd // LANES
    wide = lambda col: jnp.concatenate([col] * reps, axis=1)

    m_heads, decays = [], []
    for h in range(nh):
        m_prev = m_scr[h]
        x = jnp.where(causal, g_row[h:h + 1, :], -jnp.inf)
        m_run = jnp.maximum(jnp.max(x, axis=1, keepdims=True), m_prev)
        m_run_b = jnp.broadcast_to(m_run, (lc, LANES))
        if lc % LANES == 0:
            d_scr[h] = jnp.exp(x - jnp.concatenate([m_run_b] * (lc // LANES), axis=1))
        else:
            d_scr[h] = jnp.exp(x - m_run)
        m_last = m_run[lc - 1:lc, :]
        m_row = jnp.broadcast_to(b_col[:, h:h + 1], (lc, LANES)) + m_run_b
        f_scr[h] = jnp.exp(-m_row)
        fb_scr[h, 0] = jnp.exp(m_prev - m_run_b)
        fb_scr[h, 1] = jnp.exp(jnp.broadcast_to(g_col[:, h:h + 1], (lc, LANES)) - m_last)
        decays.append(jnp.exp(m_prev - m_last))
        m_heads.append(b_col[lc - 1:lc, h:h + 1] + m_last)

    h_heads, c_heads = [], []
    for h in range(nh):
        sl = slice(h * hd, (h + 1) * hd)
        q = q_ref[:, sl]
        if scale_is_pow2:
            k = k_ref[:, sl] * scale
        else:
            k = (k_ref[:, sl].astype(F32) * scale).astype(BF16)
        v_aug = jnp.concatenate([v_ref[:, sl], ones_v], axis=1)
        c_prev = c_scr[h]

        qk = lax.dot_general(q, k, (((1,), (1,)), ((), ())), preferred_element_type=F32)
        s = (qk * d_scr[h]).astype(BF16)
        qw = (q.astype(F32) * wide(fb_scr[h, 0])).astype(BF16)
        tot = _dot(jnp.concatenate([qw, s], axis=1),
                   jnp.concatenate([c_prev.astype(BF16), v_aug], axis=0))
        inv = 1.0 / jnp.maximum(jnp.abs(tot[:, hd:]), f_scr[h])
        hg = tot[:, :hd] * wide(inv) * jax.nn.sigmoid(o_ref[:, sl].astype(F32))
        h_heads.append((_layer_norm(hg) * gnw_ref[:, sl]).astype(BF16))

        kw = (k.astype(F32) * wide(fb_scr[h, 1])).astype(BF16)
        c_heads.append(decays[h] * c_prev + lax.dot_general(
            kw, v_aug, (((0,), (0,)), ((), ())), preferred_element_type=F32))

    h_ref[...] = jnp.concatenate(h_heads, axis=1)
    c_scr[...] = jnp.stack(c_heads, axis=0)
    m_scr[...] = jnp.stack(m_heads, axis=0)

    @pl.when(ci == pl.num_programs(1) - 1)
    def _():
        c_out[0] = c_scr[:, :, 0:hd]
        n_out[0] = jnp.stack([jnp.transpose(c_scr[h, :, hd:hd + LANES])[0:1, :] for h in range(nh)], axis=0)
        m_out[0] = m_scr[...]


def _mlstm(qkvo, ifg, gate_bias, gn_w, state, b, l, lc, nh):
    mw = gn_w.shape[1]
    hd = mw // nh
    assert hd % LANES == 0
    nc = l // lc
    rows = lambda bi, ci: (bi * nc + ci, 0)
    col = lambda j: (lambda bi, ci: (bi * nc + ci, j))
    st4 = lambda bi, ci: (bi, 0, 0, 0)
    zero_init = state is None
    in_specs = [pl.BlockSpec((lc, mw), col(0)), pl.BlockSpec((lc, mw), col(1)),
                pl.BlockSpec((lc, mw), col(2)), pl.BlockSpec((lc, mw), col(3)),
                pl.BlockSpec((lc, LANES), rows),
                pl.BlockSpec((1, LANES), lambda bi, ci: (0, 0)),
                pl.BlockSpec((1, mw), lambda bi, ci: (0, 0))]
    args = [qkvo, qkvo, qkvo, qkvo, ifg, gate_bias, gn_w]
    if not zero_init:
        c0, n0, m0 = state
        in_specs += [pl.BlockSpec((1, nh, hd, hd), st4), pl.BlockSpec((1, nh, 1, hd), st4),
                     pl.BlockSpec((1, nh, 1, 1), st4)]
        args += [c0, n0.reshape(b, nh, 1, hd), m0.reshape(b, nh, 1, 1)]
    nbytes = (2 * 5 * lc * mw * 2 + 5 * nh * hd * (hd + LANES) * 4 + nh * lc * (lc + 3 * LANES) * 4
              + 16 * lc * max(lc, hd + LANES) * 4)
    h, c, n, m = pl.pallas_call(
        functools.partial(_mlstm_kernel, nh=nh, hd=hd, zero_init=zero_init),
        grid=(b, nc),
        in_specs=in_specs,
        out_specs=[pl.BlockSpec((lc, mw), rows), pl.BlockSpec((1, nh, hd, hd), st4),
                   pl.BlockSpec((1, nh, 1, hd), st4), pl.BlockSpec((1, nh, 1, 1), st4)],
        out_shape=[jax.ShapeDtypeStruct((b * l, mw), BF16),
                   jax.ShapeDtypeStruct((b, nh, hd, hd), F32),
                   jax.ShapeDtypeStruct((b, nh, 1, hd), F32),
                   jax.ShapeDtypeStruct((b, nh, 1, 1), F32)],
        scratch_shapes=[pltpu.VMEM((nh, hd, hd + LANES), F32), pltpu.VMEM((nh, 1, 1), F32),
                        pltpu.VMEM((nh, lc, lc), F32), pltpu.VMEM((nh, lc, LANES), F32),
                        pltpu.VMEM((nh, 2, lc, LANES), F32)],
        compiler_params=_params(nbytes, 2),
        name="mlstm",
    )(*args)
    return h, c, n.reshape(b, nh, hd), m.reshape(b, nh)


def _mix_kernel(*refs, start_pos, alpha, has_prefix):
    if has_prefix:
        (p_ref, pre_ref, bo_ref, ga_ref, gb_ref, x_ref, g1_ref, sh2_ref, sc2_ref, wpool_ref, psc_ref,
         wpa_ref, wpb_ref, wout_ref, l1g_ref, l1b_ref, x1_ref, u2_ref, ext_scr) = refs
    else:
        (p_ref, bo_ref, ga_ref, gb_ref, x_ref, g1_ref, sh2_ref, sc2_ref, wpool_ref, psc_ref,
         wpa_ref, wpb_ref, wout_ref, l1g_ref, l1b_ref, x1_ref, u2_ref, ext_scr) = refs
    nb, lt, pw = p_ref.shape
    d = x_ref.shape[2]
    grp = pw // len(POOL_WINDOWS)
    ti = pl.program_id(1)

    @pl.when(ti == 0)
    def _():
        ext_scr[:, 0:POOL_HALO, :] = jnp.zeros((nb, POOL_HALO, pw), F32)
        if has_prefix:
            ext_scr[:, POOL_HALO - POOL_STATE:POOL_HALO, :] = pre_ref[...]

    @pl.when(ti > 0)
    def _():
        ext_scr[:, 0:POOL_HALO, :] = ext_scr[:, lt:lt + POOL_HALO, :]

    ext_scr[:, POOL_HALO:POOL_HALO + lt, :] = p_ref[...]

    pos = start_pos + ti * lt + lax.broadcasted_iota(jnp.int32, (lt, LANES), 0)
    branch = []
    for g, w in enumerate(POOL_WINDOWS):
        cs = slice(g * grp, (g + 1) * grp)
        inv = 1.0 / jnp.minimum(pos + 1, w).astype(F32)
        inv = jnp.concatenate([inv] * (grp // LANES), axis=1)
        ys = []
        for bi in range(nb):
            ext = ext_scr[bi, :, cs]
            acc = ext
            shift = 1
            while shift < w:
                acc = acc + pltpu.roll(acc, shift, 0)
                shift *= 2
            ys.append(acc[POOL_HALO:] * inv - ext[POOL_HALO:])
        y = jnp.concatenate(ys, axis=0).astype(BF16)
        branch.append((_dot(y, wpool_ref[g]) * psc_ref[:, cs]).astype(BF16))
    a_out = jnp.concatenate(branch, axis=1)

    pa = _dot(a_out, wpa_ref[...])
    pb = _dot(bo_ref[...], wpb_ref[...])
    merged = (jax.nn.sigmoid(ga_ref[...].astype(F32)) * pa
              + jax.nn.sigmoid(gb_ref[...].astype(F32)) * pb).astype(BF16)
    t = _dot(merged, wout_ref[...]).reshape(nb, lt, d)
    x1 = _layer_norm(alpha * x_ref[...] + g1_ref[...] * t) * l1g_ref[...] + l1b_ref[...]
    x1_ref[...] = x1
    u2 = _layer_norm(x1) * (1.0 + sc2_ref[...]) + sh2_ref[...]
    u2_ref[...] = u2.reshape(nb * lt, d).astype(BF16)


def _mix(p, prefix, b_out, gg, x, mod4, w_pool, pool_scale, w_pa, w_pb, w_out, ln1_g, ln1_b,
         nb, lt, start_pos, alpha):
    b, l, d = x.shape
    pw = p.shape[1]
    mw = b_out.shape[1]
    assert (pw // len(POOL_WINDOWS)) % LANES == 0
    tm = nb * lt
    nt = l // lt
    rows = lambda bi, ti: (bi * nt + ti, 0)
    seq3 = lambda bi, ti: (bi, ti, 0)
    const2 = lambda bi, ti: (0, 0)
    modk = lambda kk: (lambda bi, ti: (bi, kk, 0, 0))
    has_prefix = prefix is not None
    single = pl.Buffered(1)
    in_specs = [pl.BlockSpec((nb, lt, pw), seq3)]
    args = [p.reshape(b, l, pw)]
    if has_prefix:
        in_specs.append(pl.BlockSpec((nb, POOL_STATE, pw), lambda bi, ti: (bi, 0, 0)))
        args.append(prefix)
    in_specs += [pl.BlockSpec((tm, mw), rows),
                 pl.BlockSpec((tm, d), rows),
                 pl.BlockSpec((tm, d), lambda bi, ti: (bi * nt + ti, 1)),
                 pl.BlockSpec((nb, lt, d), seq3),
                 pl.BlockSpec((nb, None, 1, d), modk(2)),
                 pl.BlockSpec((nb, None, 1, d), modk(3)),
                 pl.BlockSpec((nb, None, 1, d), modk(4)),
                 pl.BlockSpec(w_pool.shape, lambda bi, ti: (0, 0, 0), pipeline_mode=single),
                 pl.BlockSpec((1, pw), const2),
                 pl.BlockSpec(w_pa.shape, const2, pipeline_mode=single),
                 pl.BlockSpec(w_pb.shape, const2, pipeline_mode=single),
                 pl.BlockSpec(w_out.shape, const2, pipeline_mode=single),
                 pl.BlockSpec((1, d), const2),
                 pl.BlockSpec((1, d), const2)]
    args += [b_out, gg, gg, x, mod4, mod4, mod4, w_pool, pool_scale, w_pa, w_pb, w_out, ln1_g, ln1_b]
    wbytes = 2 * (w_pool.size + w_pa.size + w_pb.size + w_out.size)
    nbytes = (wbytes + 2 * tm * (pw * 4 + mw * 2 + 2 * d * 2 + d * 4) + 2 * tm * (d * 4 + d * 2)
              + nb * (POOL_HALO + lt) * pw * 4 + 6 * tm * d * 4)
    x1, u2 = pl.pallas_call(
        functools.partial(_mix_kernel, start_pos=start_pos, alpha=alpha, has_prefix=has_prefix),
        grid=(b // nb, nt),
        in_specs=in_specs,
        out_specs=[pl.BlockSpec((nb, lt, d), seq3), pl.BlockSpec((tm, d), rows)],
        out_shape=[jax.ShapeDtypeStruct((b, l, d), F32), jax.ShapeDtypeStruct((b * l, d), BF16)],
        scratch_shapes=[pltpu.VMEM((nb, POOL_HALO + lt, pw), F32)],
        compiler_params=_params(nbytes, 2),
        name="mix",
    )(*args)
    return x1, u2


def _ffn_up_kernel(u_ref, wg_ref, wu_ref, h_ref, wg_scr, wu_scr):
    @pl.when(pl.program_id(1) == 0)
    def _():
        wg_scr[...] = wg_ref[...].astype(BF16)
        wu_scr[...] = wu_ref[...].astype(BF16)

    half = u_ref.shape[0] // 2
    parts = [slice(0, half), slice(half, 2 * half)] if half % 512 == 0 else [slice(None)]
    for rows in parts:
        u = u_ref[rows, :]
        hg = _dot(u, wg_scr[...])
        hu = _dot(u, wu_scr[...])
        h_ref[rows, :] = (hg * jax.nn.sigmoid(hg) * hu).astype(BF16)


def _ffn_up(u2, w_gate, w_up, tm, tn):
    m, d = u2.shape
    ff = w_gate.shape[1]
    nbytes = 2 * tm * d * 2 + 2 * (2 * 4 + 2) * d * tn + 2 * tm * tn * 2 + 4 * tm * tn * 4
    return pl.pallas_call(
        _ffn_up_kernel,
        grid=(ff // tn, m // tm),
        in_specs=[pl.BlockSpec((tm, d), lambda j, i: (i, 0)),
                  pl.BlockSpec((d, tn), lambda j, i: (0, j)),
                  pl.BlockSpec((d, tn), lambda j, i: (0, j))],
        out_specs=pl.BlockSpec((tm, tn), lambda j, i: (i, j)),
        out_shape=jax.ShapeDtypeStruct((m, ff), BF16),
        scratch_shapes=[pltpu.VMEM((d, tn), BF16), pltpu.VMEM((d, tn), BF16)],
        compiler_params=_params(nbytes, 2),
        name="ffn_up",
    )(u2, w_gate, w_up)


def _ffn_down_kernel(h_ref, wd_ref, x1_ref, g2_ref, l2g_ref, l2b_ref, y_ref, *, alpha):
    nb, lt, d = x1_ref.shape
    for bs, ts, rs in _row_parts(nb, lt):
        f = _dot(h_ref[rs, :], wd_ref[...]).reshape(bs.stop - bs.start, ts.stop - ts.start, d)
        z = alpha * x1_ref[bs, ts, :] + g2_ref[bs, :, :] * f
        y_ref[bs, ts, :] = _layer_norm(z) * l2g_ref[...] + l2b_ref[...]


def _ffn_down(hidden, x1, mod4, w_down, ln2_g, ln2_b, nb, lt, alpha):
    b, l, d = x1.shape
    ff = w_down.shape[0]
    tm = nb * lt
    nt = l // lt
    const2 = lambda bi, ti: (0, 0)
    seq3 = lambda bi, ti: (bi, ti, 0)
    nbytes = 2 * tm * ff * 2 + ff * d * 2 + 4 * tm * d * 4 + 4 * tm * d * 4
    return pl.pallas_call(
        functools.partial(_ffn_down_kernel, alpha=alpha),
        grid=(b // nb, nt),
        in_specs=[pl.BlockSpec((tm, ff), lambda bi, ti: (bi * nt + ti, 0)),
                  pl.BlockSpec((ff, d), const2, pipeline_mode=pl.Buffered(1)),
                  pl.BlockSpec((nb, lt, d), seq3),
                  pl.BlockSpec((nb, None, 1, d), lambda bi, ti: (bi, 5, 0, 0)),
                  pl.BlockSpec((1, d), const2),
                  pl.BlockSpec((1, d), const2)],
        out_specs=pl.BlockSpec((nb, lt, d), seq3),
        out_shape=jax.ShapeDtypeStruct((b, l, d), F32),
        compiler_params=_params(nbytes, 2),
        name="ffn_down",
    )(hidden, w_down, x1, mod4, ln2_g, ln2_b)


def _tile(total, want):
    if total <= want:
        return total
    t = want
    while total % t or t % 8:
        t -= 1
    return t


def _stream(x, mod, state, start_pos, w, alpha):
    b, l, d = x.shape
    nh = w["nh"]
    pw = w["pool_scale"].shape[1]
    mw = w["gn_w"].shape[1]
    mod4 = mod.reshape(b, 6, 1, d)

    def tiling(rows_want):
        lt = _tile(l, rows_want)
        nb = _tile(b, max(1, rows_want // lt)) if lt == l else 1
        return nb, lt

    nb, lt = tiling(1024)
    u, p, ifg = _inproj(x, mod4, w["w_in_t"], pw, w["w_if"], nb, lt)
    tm = _tile(b * l, 2048)
    qkvo = _matmul(u, w["w_in_t"], pw, 4 * mw, tm, 1024)
    gg = _matmul(u, w["w_gg"], 0, 2 * d, tm, 1024)

    lc = _tile(l, 256)
    mstate = None if state is None else state[1:]
    b_out, c_new, n_new, m_new = _mlstm(qkvo, ifg, w["gate_bias"], w["gn_w"], mstate, b, l, lc, nh)

    nb, lt = tiling(256)
    prefix = None if state is None else state[0]
    x1, u2 = _mix(p, prefix, b_out, gg, x, mod4, w["w_pool"], w["pool_scale"], w["w_pa"], w["w_pb"],
                  w["w_out"], w["ln1_g"], w["ln1_b"], nb, lt, start_pos, alpha)

    ff = w["w_gate"].shape[1]
    hidden = _ffn_up(u2, w["w_gate"], w["w_up"], tm, 512 if ff % 512 == 0 else ff)
    nb, lt = tiling(512)
    y = _ffn_down(hidden, x1, mod4, w["w_down"], w["ln2_g"], w["ln2_b"], nb, lt, alpha)
    pool_state = p.reshape(b, l, pw)[:, l - POOL_STATE:, :]
    return y, pool_state, c_new, n_new, m_new


def kernel(x_prompt, x_sample, c_prompt, c_sample, state_pool, state_mlstm_C, state_mlstm_n, state_mlstm_m, w_ada, b_ada, w_in, b_i, b_f, w_pool, pool_scale, gn_w, w_pa, w_pb, w_out, ln1_g, ln1_b, w_gate, w_up, w_down, ln2_g, ln2_b):
    depth = w_ada.shape[0]
    alpha = (2 * depth) ** 0.25
    d = x_prompt.shape[2]
    nh = b_i.shape[1]
    pw = pool_scale.shape[1]
    mw = gn_w.shape[1]
    nbp = x_prompt.shape[0]
    assert pw % 1024 == 0 and mw % 1024 == 0 and d % 1024 == 0
    y_p, y_s = x_prompt, x_sample
    outs_p, outs_s = [], []
    for li in range(depth):
        wt = jnp.transpose(w_in[li])
        gates0 = pw + 4 * mw
        w = {
            "nh": nh,
            "w_in_t": wt,
            "w_if": jnp.pad(wt[gates0:gates0 + 2 * nh], ((0, LANES - 2 * nh), (0, 0))).astype(BF16),
            "w_gg": wt[gates0 + 2 * nh:],
            "gate_bias": jnp.pad(jnp.concatenate([b_i[li], b_f[li]]), (0, LANES - 2 * nh)).reshape(1, LANES),
            "gn_w": gn_w[li].reshape(1, mw),
            "w_pool": w_pool[li].astype(BF16),
            "pool_scale": pool_scale[li].reshape(1, pw),
            "w_pa": w_pa[li].astype(BF16),
            "w_pb": w_pb[li].astype(BF16),
            "w_out": w_out[li].astype(BF16),
            "ln1_g": ln1_g[li].reshape(1, d),
            "ln1_b": ln1_b[li].reshape(1, d),
            "w_gate": w_gate[li],
            "w_up": w_up[li],
            "w_down": w_down[li].astype(BF16),
            "ln2_g": ln2_g[li].reshape(1, d),
            "ln2_b": ln2_b[li].reshape(1, d),
        }
        mod = _ada(jnp.concatenate([c_prompt, c_sample], axis=0), w_ada[li], b_ada[li])
        res_p = _stream(y_p, mod[:nbp], None, 0, w, alpha)
        res_s = _stream(y_s, mod[nbp:], (state_pool[li], state_mlstm_C[li], state_mlstm_n[li], state_mlstm_m[li]),
                        PAST_LEN, w, alpha)
        y_p, y_s = res_p[0], res_s[0]
        outs_p.append(res_p[1:])
        outs_s.append(res_s[1:])
    stack = lambda outs, k: jnp.stack([o[k] for o in outs], axis=0)
    return (y_p, y_s,
            stack(outs_p, 0), stack(outs_p, 1), stack(outs_p, 2), stack(outs_p, 3),
            stack(outs_s, 0), stack(outs_s, 1), stack(outs_s, 2), stack(outs_s, 3))
```

```python
import functools

import jax
import jax.numpy as jnp
from jax import lax
from jax.experimental import pallas as pl
from jax.experimental.pallas import tpu as pltpu

F32 = jnp.float32
BF16 = jnp.bfloat16

LN_EPS = 1e-5
POOL_WINDOWS = (2, 4, 8, 16)
POOL_STATE = max(POOL_WINDOWS) - 1
POOL_HALO = 16
PAST_LEN = 1024
LANES = 128
VMEM_CAP_BYTES = 60 * 1024 * 1024
VMEM_MIN_BYTES = 32 * 1024 * 1024

ROWS_INPROJ = 1024
ROWS_MATMUL = 2048
COLS_MATMUL = 1024
COLS_FFN_UP = 512
COLS_ADA = 1024
ROWS_MIX = 256
ROWS_FFN_DOWN = 512
MLSTM_CHUNK = 256


def _vmem_limit(nbytes):
    return int(min(VMEM_CAP_BYTES, max(VMEM_MIN_BYTES, nbytes * 5 // 4)))


def _params(nbytes, ndims):
    return pltpu.CompilerParams(dimension_semantics=("arbitrary",) * ndims,
                                vmem_limit_bytes=_vmem_limit(nbytes))


def _layer_norm(x):
    mu = jnp.mean(x, axis=-1, keepdims=True)
    xc = x - mu
    var = jnp.mean(xc * xc, axis=-1, keepdims=True)
    return xc * lax.rsqrt(var + LN_EPS)


MIN_MATMUL_ROWS = 256


def _row_parts(nb, lt):
    if nb == 1 and lt % (2 * MIN_MATMUL_ROWS) == 0:
        h = lt // 2
        return [(slice(0, 1), slice(0, h), slice(0, h)), (slice(0, 1), slice(h, lt), slice(h, lt))]
    if nb % 2 == 0 and (nb // 2 * lt) % MIN_MATMUL_ROWS == 0:
        h = nb // 2
        return [(slice(0, h), slice(0, lt), slice(0, h * lt)), (slice(h, nb), slice(0, lt), slice(h * lt, nb * lt))]
    return [(slice(0, nb), slice(0, lt), slice(0, nb * lt))]


def _dot(a, b):
    return jnp.dot(a, b, preferred_element_type=F32)


def _dot_nt(a, bt):
    return lax.dot_general(a, bt, (((1,), (1,)), ((), ())), preferred_element_type=F32)


def _log_sigmoid(x):
    return jnp.minimum(x, 0.0) - jnp.log1p(jnp.exp(-jnp.abs(x)))


def _split3(x):
    hi = x.astype(BF16)
    r1 = x - hi.astype(F32)
    mid = r1.astype(BF16)
    lo = (r1 - mid.astype(F32)).astype(BF16)
    return hi, mid, lo


def _ada_kernel(c_ref, w_ref, b_ref, o_ref):
    c = c_ref[...]
    a = (c * jax.nn.sigmoid(c)).astype(BF16)
    o_ref[...] = _dot(a, w_ref[...].astype(BF16)) + b_ref[...]


def _ada(c, w_ada, b_ada):
    nb, d = c.shape
    n = w_ada.shape[1]
    tn = COLS_ADA if n % COLS_ADA == 0 else n
    nbytes = 2 * (d * tn * 4) + d * tn * 2 + 4 * nb * (d + tn) * 4
    return pl.pallas_call(
        _ada_kernel,
        grid=(n // tn,),
        in_specs=[pl.BlockSpec((nb, d), lambda j: (0, 0)),
                  pl.BlockSpec((d, tn), lambda j: (0, j)),
                  pl.BlockSpec((1, tn), lambda j: (0, j))],
        out_specs=pl.BlockSpec((nb, tn), lambda j: (0, j)),
        out_shape=jax.ShapeDtypeStruct((nb, n), F32),
        compiler_params=_params(nbytes, 1),
        name="ada",
    )(c, w_ada, b_ada.reshape(1, n))


def _inproj_kernel(x_ref, sh_ref, sc_ref, wp_ref, wif_ref, u_ref, p_ref, ifg_ref, wp_scr):
    nb, lt, d = x_ref.shape

    @pl.when((pl.program_id(0) == 0) & (pl.program_id(1) == 0))
    def _():
        wp_scr[...] = wp_ref[...].astype(BF16)

    for bs, ts, rs in _row_parts(nb, lt):
        u = _layer_norm(x_ref[bs, ts, :]) * (1.0 + sc_ref[bs, :, :]) + sh_ref[bs, :, :]
        ub = u.reshape(rs.stop - rs.start, d).astype(BF16)
        u_ref[rs, :] = ub
        p_ref[rs, :] = _dot_nt(ub, wp_scr[...])
        ifg_ref[rs, :] = _dot_nt(ub, wif_ref[...])


def _inproj(x, mod4, w_rows, pw, w_if, nb, lt):
    b, l, d = x.shape
    tm = nb * lt
    nt = l // lt
    rows = lambda bi, ti: (bi * nt + ti, 0)
    nbytes = (2 * tm * d * 4 + 2 * tm * d * 2 + 2 * tm * pw * 4 + d * pw * (4 + 2)
              + 3 * tm * d * 4 + tm * pw * 4)
    return pl.pallas_call(
        _inproj_kernel,
        grid=(b // nb, nt),
        in_specs=[pl.BlockSpec((nb, lt, d), lambda bi, ti: (bi, ti, 0)),
                  pl.BlockSpec((nb, None, 1, d), lambda bi, ti: (bi, 0, 0, 0)),
                  pl.BlockSpec((nb, None, 1, d), lambda bi, ti: (bi, 1, 0, 0)),
                  pl.BlockSpec((pw, d), lambda bi, ti: (0, 0), pipeline_mode=pl.Buffered(1)),
                  pl.BlockSpec((LANES, d), lambda bi, ti: (0, 0))],
        out_specs=[pl.BlockSpec((tm, d), rows),
                   pl.BlockSpec((tm, pw), rows),
                   pl.BlockSpec((tm, LANES), rows)],
        out_shape=[jax.ShapeDtypeStruct((b * l, d), BF16),
                   jax.ShapeDtypeStruct((b * l, pw), F32),
                   jax.ShapeDtypeStruct((b * l, LANES), F32)],
        scratch_shapes=[pltpu.VMEM((pw, d), BF16)],
        compiler_params=_params(nbytes, 2),
        name="inproj",
    )(x, mod4, mod4, w_rows, w_if)


def _matmul_kernel(a_ref, b_ref, o_ref, *w_scr):
    if w_scr:
        @pl.when(pl.program_id(1) == 0)
        def _():
            w_scr[0][...] = b_ref[...].astype(BF16)
        w = w_scr[0][...]
    else:
        w = b_ref[...]
    half = a_ref.shape[0] // 2
    if half % (2 * MIN_MATMUL_ROWS) == 0:
        o_ref[:half, :] = _dot_nt(a_ref[:half, :], w).astype(o_ref.dtype)
        o_ref[half:, :] = _dot_nt(a_ref[half:, :], w).astype(o_ref.dtype)
    else:
        o_ref[...] = _dot_nt(a_ref[...], w).astype(o_ref.dtype)


def _matmul(a, b, row0, n, tm, tn):
    m, k = a.shape
    wsize = b.dtype.itemsize
    scratch = [pltpu.VMEM((tn, k), BF16)] if b.dtype == F32 else []
    nbytes = 2 * tm * k * 2 + (2 * wsize + 2) * k * tn + 2 * tm * tn * 2 + tm * tn * 4
    if row0 % tn == 0:
        w_spec = pl.BlockSpec((tn, k), lambda j, i: (j + row0 // tn, 0))
    else:
        assert row0 % 8 == 0
        w_spec = pl.BlockSpec((pl.Element(tn), pl.Element(k)),
                              lambda j, i: (pl.multiple_of(row0 + j * tn, 8), 0))
    return pl.pallas_call(
        _matmul_kernel,
        grid=(n // tn, m // tm),
        in_specs=[pl.BlockSpec((tm, k), lambda j, i: (i, 0)), w_spec],
        out_specs=pl.BlockSpec((tm, tn), lambda j, i: (i, j)),
        out_shape=jax.ShapeDtypeStruct((m, n), BF16),
        scratch_shapes=scratch,
        compiler_params=_params(nbytes, 2),
        name="bigproj",
    )(a, b)


def _mlstm_kernel(*refs, nh, hd, zero_init):
    if zero_init:
        (q_ref, k_ref, v_ref, o_ref, ifg_ref, bias_ref, gnw_ref,
         h_ref, c_out, n_out, m_out, c_scr, m_scr, d_scr, f_scr, fb_scr) = refs
    else:
        (q_ref, k_ref, v_ref, o_ref, ifg_ref, bias_ref, gnw_ref, c0_ref, n0_ref, m0_ref,
         h_ref, c_out, n_out, m_out, c_scr, m_scr, d_scr, f_scr, fb_scr) = refs
    lc = q_ref.shape[0]
    ci = pl.program_id(1)

    @pl.when(ci == 0)
    def _():
        if zero_init:
            c_scr[...] = jnp.zeros_like(c_scr)
            m_scr[...] = jnp.zeros_like(m_scr)
        else:
            c_scr[:, :, 0:hd] = c0_ref[0]
            for h in range(nh):
                c_scr[h, :, hd:hd + LANES] = jnp.transpose(jnp.broadcast_to(n0_ref[0, h], (LANES, hd)))
            m_scr[...] = m0_ref[0]

    gates = ifg_ref[...] + bias_ref[...]
    lf = _log_sigmoid(gates)
    row_t = lax.broadcasted_iota(jnp.int32, (lc, lc), 0)
    col_s = lax.broadcasted_iota(jnp.int32, (lc, lc), 1)
    causal = col_s <= row_t
    tril = jnp.where(causal, 1.0, 0.0).astype(BF16)
    b_all = sum(_dot(tril, piece) for piece in _split3(lf))
    b_col = pltpu.roll(b_all, LANES - nh, 1)
    g_col = gates - b_col
    g_row = jnp.transpose(g_col)
    scale = hd ** -0.5
    scale_is_pow2 = (hd & (hd - 1)) == 0 and (hd.bit_length() - 1) % 2 == 0
    ones_v = jnp.ones((lc, LANES), BF16)
    reps = hd // LANES
    wide = lambda col: jnp.concatenate([col] * reps, axis=1)

    m_heads, decays, h_heads, c_heads = {}, {}, {}, {}

    def gate_phase(h):
        m_prev = m_scr[h]
        x = jnp.where(causal, g_row[h:h + 1, :], -jnp.inf)
        m_run = jnp.maximum(jnp.max(x, axis=1, keepdims=True), m_prev)
        m_run_b = jnp.broadcast_to(m_run, (lc, LANES))
        if lc % LANES == 0:
            d_scr[h] = jnp.exp(x - jnp.concatenate([m_run_b] * (lc // LANES), axis=1))
        else:
            d_scr[h] = jnp.exp(x - m_run)
        m_last = m_run[lc - 1:lc, :]
        m_row = jnp.broadcast_to(b_col[:, h:h + 1], (lc, LANES)) + m_run_b
        f_scr[h] = jnp.exp(-m_row)
        fb_scr[h, 0] = jnp.exp(m_prev - m_run_b)
        fb_scr[h, 1] = jnp.exp(jnp.broadcast_to(g_col[:, h:h + 1], (lc, LANES)) - m_last)
        decays[h] = jnp.exp(m_prev - m_last)
        m_heads[h] = b_col[lc - 1:lc, h:h + 1] + m_last

    def matmul_phase(h):
        sl = slice(h * hd, (h + 1) * hd)
        q = q_ref[:, sl]
        if scale_is_pow2:
            k = k_ref[:, sl] * scale
        else:
            k = (k_ref[:, sl].astype(F32) * scale).astype(BF16)
        v_aug = jnp.concatenate([v_ref[:, sl], ones_v], axis=1)
        c_prev = c_scr[h]

        qk = lax.dot_general(q, k, (((1,), (1,)), ((), ())), preferred_element_type=F32)
        s = (qk * d_scr[h]).astype(BF16)
        qw = (q.astype(F32) * wide(fb_scr[h, 0])).astype(BF16)
        tot = _dot(jnp.concatenate([qw, s], axis=1),
                   jnp.concatenate([c_prev.astype(BF16), v_aug], axis=0))
        inv = 1.0 / jnp.maximum(jnp.abs(tot[:, hd:]), f_scr[h])
        hg = tot[:, :hd] * wide(inv) * jax.nn.sigmoid(o_ref[:, sl].astype(F32))
        h_heads[h] = (_layer_norm(hg) * gnw_ref[:, sl]).astype(BF16)

        kw = (k.astype(F32) * wide(fb_scr[h, 1])).astype(BF16)
        c_heads[h] = decays[h] * c_prev + lax.dot_general(
            kw, v_aug, (((0,), (0,)), ((), ())), preferred_element_type=F32)

    for h in range(nh):
        gate_phase(h)
    for h in range(nh):
        matmul_phase(h)

    h_ref[...] = jnp.concatenate([h_heads[h] for h in range(nh)], axis=1)
    c_scr[...] = jnp.stack([c_heads[h] for h in range(nh)], axis=0)
    m_scr[...] = jnp.stack([m_heads[h] for h in range(nh)], axis=0)

    @pl.when(ci == pl.num_programs(1) - 1)
    def _():
        c_out[0] = c_scr[:, :, 0:hd]
        n_out[0] = jnp.stack([jnp.transpose(c_scr[h, :, hd:hd + LANES])[0:1, :] for h in range(nh)], axis=0)
        m_out[0] = m_scr[...]


def _mlstm(qkvo, ifg, gate_bias, gn_w, state, b, l, lc, nh):
    mw = gn_w.shape[1]
    hd = mw // nh
    assert hd % LANES == 0
    nc = l // lc
    rows = lambda bi, ci: (bi * nc + ci, 0)
    col = lambda j: (lambda bi, ci: (bi * nc + ci, j))
    st4 = lambda bi, ci: (bi, 0, 0, 0)
    zero_init = state is None
    in_specs = [pl.BlockSpec((lc, mw), col(0)), pl.BlockSpec((lc, mw), col(1)),
                pl.BlockSpec((lc, mw), col(2)), pl.BlockSpec((lc, mw), col(3)),
                pl.BlockSpec((lc, LANES), rows),
                pl.BlockSpec((1, LANES), lambda bi, ci: (0, 0)),
                pl.BlockSpec((1, mw), lambda bi, ci: (0, 0))]
    args = [qkvo, qkvo, qkvo, qkvo, ifg, gate_bias, gn_w]
    if not zero_init:
        c0, n0, m0 = state
        in_specs += [pl.BlockSpec((1, nh, hd, hd), st4), pl.BlockSpec((1, nh, 1, hd), st4),
                     pl.BlockSpec((1, nh, 1, 1), st4)]
        args += [c0, n0.reshape(b, nh, 1, hd), m0.reshape(b, nh, 1, 1)]
    nbytes = (2 * 5 * lc * mw * 2 + 5 * nh * hd * (hd + LANES) * 4 + nh * lc * (lc + 3 * LANES) * 4
              + 16 * lc * max(lc, hd + LANES) * 4)
    h, c, n, m = pl.pallas_call(
        functools.partial(_mlstm_kernel, nh=nh, hd=hd, zero_init=zero_init),
        grid=(b, nc),
        in_specs=in_specs,
        out_specs=[pl.BlockSpec((lc, mw), rows), pl.BlockSpec((1, nh, hd, hd), st4),
                   pl.BlockSpec((1, nh, 1, hd), st4), pl.BlockSpec((1, nh, 1, 1), st4)],
        out_shape=[jax.ShapeDtypeStruct((b * l, mw), BF16),
                   jax.ShapeDtypeStruct((b, nh, hd, hd), F32),
                   jax.ShapeDtypeStruct((b, nh, 1, hd), F32),
                   jax.ShapeDtypeStruct((b, nh, 1, 1), F32)],
        scratch_shapes=[pltpu.VMEM((nh, hd, hd + LANES), F32), pltpu.VMEM((nh, 1, 1), F32),
                        pltpu.VMEM((nh, lc, lc), F32), pltpu.VMEM((nh, lc, LANES), F32),
                        pltpu.VMEM((nh, 2, lc, LANES), F32)],
        compiler_params=_params(nbytes, 2),
        name="mlstm",
    )(*args)
    return h, c, n.reshape(b, nh, hd), m.reshape(b, nh)


def _mix_kernel(*refs, start_pos, alpha, has_prefix):
    if has_prefix:
        (p_ref, pre_ref, bo_ref, ga_ref, gb_ref, x_ref, g1_ref, sh2_ref, sc2_ref, wpool_ref, psc_ref,
         wpa_ref, wpb_ref, wout_ref, l1g_ref, l1b_ref, x1_ref, u2_ref, ext_scr) = refs
    else:
        (p_ref, bo_ref, ga_ref, gb_ref, x_ref, g1_ref, sh2_ref, sc2_ref, wpool_ref, psc_ref,
         wpa_ref, wpb_ref, wout_ref, l1g_ref, l1b_ref, x1_ref, u2_ref, ext_scr) = refs
    nb, lt, pw = p_ref.shape
    d = x_ref.shape[2]
    grp = pw // len(POOL_WINDOWS)
    ti = pl.program_id(1)

    @pl.when(ti == 0)
    def _():
        ext_scr[:, 0:POOL_HALO, :] = jnp.zeros((nb, POOL_HALO, pw), F32)
        if has_prefix:
            ext_scr[:, POOL_HALO - POOL_STATE:POOL_HALO, :] = pre_ref[...]

    @pl.when(ti > 0)
    def _():
        ext_scr[:, 0:POOL_HALO, :] = ext_scr[:, lt:lt + POOL_HALO, :]

    ext_scr[:, POOL_HALO:POOL_HALO + lt, :] = p_ref[...]

    pos = start_pos + ti * lt + lax.broadcasted_iota(jnp.int32, (lt, LANES), 0)
    branch = []
    for g, w in enumerate(POOL_WINDOWS):
        cs = slice(g * grp, (g + 1) * grp)
        inv = 1.0 / jnp.minimum(pos + 1, w).astype(F32)
        inv = jnp.concatenate([inv] * (grp // LANES), axis=1)
        ys = []
        for bi in range(nb):
            ext = ext_scr[bi, :, cs]
            acc = ext
            shift = 1
            while shift < w:
                acc = acc + pltpu.roll(acc, shift, 0)
                shift *= 2
            ys.append(acc[POOL_HALO:] * inv - ext[POOL_HALO:])
        y = jnp.concatenate(ys, axis=0).astype(BF16)
        branch.append((_dot(y, wpool_ref[g]) * psc_ref[:, cs]).astype(BF16))
    a_out = jnp.concatenate(branch, axis=1)

    pa = _dot(a_out, wpa_ref[...])
    pb = _dot(bo_ref[...], wpb_ref[...])
    merged = (jax.nn.sigmoid(ga_ref[...].astype(F32)) * pa
              + jax.nn.sigmoid(gb_ref[...].astype(F32)) * pb).astype(BF16)
    t = _dot(merged, wout_ref[...]).reshape(nb, lt, d)
    x1 = _layer_norm(alpha * x_ref[...] + g1_ref[...] * t) * l1g_ref[...] + l1b_ref[...]
    x1_ref[...] = x1
    u2 = _layer_norm(x1) * (1.0 + sc2_ref[...]) + sh2_ref[...]
    u2_ref[...] = u2.reshape(nb * lt, d).astype(BF16)


def _mix(p, prefix, b_out, gg, x, mod4, w_pool, pool_scale, w_pa, w_pb, w_out, ln1_g, ln1_b,
         nb, lt, start_pos, alpha):
    b, l, d = x.shape
    pw = p.shape[1]
    mw = b_out.shape[1]
    assert (pw // len(POOL_WINDOWS)) % LANES == 0
    tm = nb * lt
    nt = l // lt
    rows = lambda bi, ti: (bi * nt + ti, 0)
    seq3 = lambda bi, ti: (bi, ti, 0)
    const2 = lambda bi, ti: (0, 0)
    modk = lambda kk: (lambda bi, ti: (bi, kk, 0, 0))
    has_prefix = prefix is not None
    single = pl.Buffered(1)
    in_specs = [pl.BlockSpec((nb, lt, pw), seq3)]
    args = [p.reshape(b, l, pw)]
    if has_prefix:
        in_specs.append(pl.BlockSpec((nb, POOL_STATE, pw), lambda bi, ti: (bi, 0, 0)))
        args.append(prefix)
    in_specs += [pl.BlockSpec((tm, mw), rows),
                 pl.BlockSpec((tm, d), rows),
                 pl.BlockSpec((tm, d), lambda bi, ti: (bi * nt + ti, 1)),
                 pl.BlockSpec((nb, lt, d), seq3),
                 pl.BlockSpec((nb, None, 1, d), modk(2)),
                 pl.BlockSpec((nb, None, 1, d), modk(3)),
                 pl.BlockSpec((nb, None, 1, d), modk(4)),
                 pl.BlockSpec(w_pool.shape, lambda bi, ti: (0, 0, 0), pipeline_mode=single),
                 pl.BlockSpec((1, pw), const2),
                 pl.BlockSpec(w_pa.shape, const2, pipeline_mode=single),
                 pl.BlockSpec(w_pb.shape, const2, pipeline_mode=single),
                 pl.BlockSpec(w_out.shape, const2, pipeline_mode=single),
                 pl.BlockSpec((1, d), const2),
                 pl.BlockSpec((1, d), const2)]
    args += [b_out, gg, gg, x, mod4, mod4, mod4, w_pool, pool_scale, w_pa, w_pb, w_out, ln1_g, ln1_b]
    wbytes = 2 * (w_pool.size + w_pa.size + w_pb.size + w_out.size)
    nbytes = (wbytes + 2 * tm * (pw * 4 + mw * 2 + 2 * d * 2 + d * 4) + 2 * tm * (d * 4 + d * 2)
              + nb * (POOL_HALO + lt) * pw * 4 + 6 * tm * d * 4)
    x1, u2 = pl.pallas_call(
        functools.partial(_mix_kernel, start_pos=start_pos, alpha=alpha, has_prefix=has_prefix),
        grid=(b // nb, nt),
        in_specs=in_specs,
        out_specs=[pl.BlockSpec((nb, lt, d), seq3), pl.BlockSpec((tm, d), rows)],
        out_shape=[jax.ShapeDtypeStruct((b, l, d), F32), jax.ShapeDtypeStruct((b * l, d), BF16)],
        scratch_shapes=[pltpu.VMEM((nb, POOL_HALO + lt, pw), F32)],
        compiler_params=_params(nbytes, 2),
        name="mix",
    )(*args)
    return x1, u2


def _ffn_up_kernel(u_ref, wg_ref, wu_ref, h_ref, wg_scr, wu_scr):
    @pl.when(pl.program_id(1) == 0)
    def _():
        wg_scr[...] = wg_ref[...].astype(BF16)
        wu_scr[...] = wu_ref[...].astype(BF16)

    half = u_ref.shape[0] // 2
    parts = [slice(0, half), slice(half, 2 * half)] if half % (2 * MIN_MATMUL_ROWS) == 0 else [slice(None)]
    for rows in parts:
        u = u_ref[rows, :]
        hg = _dot(u, wg_scr[...])
        hu = _dot(u, wu_scr[...])
        h_ref[rows, :] = (hg * jax.nn.sigmoid(hg) * hu).astype(BF16)


def _ffn_up(u2, w_gate, w_up, tm, tn):
    m, d = u2.shape
    ff = w_gate.shape[1]
    nbytes = 2 * tm * d * 2 + 2 * (2 * 4 + 2) * d * tn + 2 * tm * tn * 2 + 4 * tm * tn * 4
    return pl.pallas_call(
        _ffn_up_kernel,
        grid=(ff // tn, m // tm),
        in_specs=[pl.BlockSpec((tm, d), lambda j, i: (i, 0)),
                  pl.BlockSpec((d, tn), lambda j, i: (0, j)),
                  pl.BlockSpec((d, tn), lambda j, i: (0, j))],
        out_specs=pl.BlockSpec((tm, tn), lambda j, i: (i, j)),
        out_shape=jax.ShapeDtypeStruct((m, ff), BF16),
        scratch_shapes=[pltpu.VMEM((d, tn), BF16), pltpu.VMEM((d, tn), BF16)],
        compiler_params=_params(nbytes, 2),
        name="ffn_up",
    )(u2, w_gate, w_up)


def _ffn_down_kernel(h_ref, wd_ref, x1_ref, g2_ref, l2g_ref, l2b_ref, y_ref, *, alpha):
    nb, lt, d = x1_ref.shape
    for bs, ts, rs in _row_parts(nb, lt):
        f = _dot(h_ref[rs, :], wd_ref[...]).reshape(bs.stop - bs.start, ts.stop - ts.start, d)
        z = alpha * x1_ref[bs, ts, :] + g2_ref[bs, :, :] * f
        y_ref[bs, ts, :] = _layer_norm(z) * l2g_ref[...] + l2b_ref[...]


def _ffn_down(hidden, x1, mod4, w_down, ln2_g, ln2_b, nb, lt, alpha):
    b, l, d = x1.shape
    ff = w_down.shape[0]
    tm = nb * lt
    nt = l // lt
    const2 = lambda bi, ti: (0, 0)
    seq3 = lambda bi, ti: (bi, ti, 0)
    nbytes = 2 * tm * ff * 2 + ff * d * 2 + 4 * tm * d * 4 + 4 * tm * d * 4
    return pl.pallas_call(
        functools.partial(_ffn_down_kernel, alpha=alpha),
        grid=(b // nb, nt),
        in_specs=[pl.BlockSpec((tm, ff), lambda bi, ti: (bi * nt + ti, 0)),
                  pl.BlockSpec((ff, d), const2, pipeline_mode=pl.Buffered(1)),
                  pl.BlockSpec((nb, lt, d), seq3),
                  pl.BlockSpec((nb, None, 1, d), lambda bi, ti: (bi, 5, 0, 0)),
                  pl.BlockSpec((1, d), const2),
                  pl.BlockSpec((1, d), const2)],
        out_specs=pl.BlockSpec((nb, lt, d), seq3),
        out_shape=jax.ShapeDtypeStruct((b, l, d), F32),
        compiler_params=_params(nbytes, 2),
        name="ffn_down",
    )(hidden, w_down, x1, mod4, ln2_g, ln2_b)


def _tile(total, want):
    if total <= want:
        return total
    t = want
    while total % t or t % 8:
        t -= 1
    return t


def _stream(x, mod, state, start_pos, w, alpha):
    b, l, d = x.shape
    nh = w["nh"]
    pw = w["pool_scale"].shape[1]
    mw = w["gn_w"].shape[1]
    mod4 = mod.reshape(b, 6, 1, d)

    def tiling(rows_want):
        lt = _tile(l, rows_want)
        nb = _tile(b, max(1, rows_want // lt)) if lt == l else 1
        return nb, lt

    nb, lt = tiling(ROWS_INPROJ)
    u, p, ifg = _inproj(x, mod4, w["w_in_t"], pw, w["w_if"], nb, lt)
    tm = _tile(b * l, ROWS_MATMUL)
    qkvo = _matmul(u, w["w_in_t"], pw, 4 * mw, tm, COLS_MATMUL)
    gg = _matmul(u, w["w_in_t"], pw + 4 * mw + 2 * nh, 2 * d, tm, COLS_MATMUL)

    lc = _tile(l, MLSTM_CHUNK)
    mstate = None if state is None else state[1:]
    b_out, c_new, n_new, m_new = _mlstm(qkvo, ifg, w["gate_bias"], w["gn_w"], mstate, b, l, lc, nh)

    nb, lt = tiling(ROWS_MIX)
    prefix = None if state is None else state[0]
    x1, u2 = _mix(p, prefix, b_out, gg, x, mod4, w["w_pool"], w["pool_scale"], w["w_pa"], w["w_pb"],
                  w["w_out"], w["ln1_g"], w["ln1_b"], nb, lt, start_pos, alpha)

    ff = w["w_gate"].shape[1]
    hidden = _ffn_up(u2, w["w_gate"], w["w_up"], tm, COLS_FFN_UP if ff % COLS_FFN_UP == 0 else ff)
    nb, lt = tiling(ROWS_FFN_DOWN)
    y = _ffn_down(hidden, x1, mod4, w["w_down"], w["ln2_g"], w["ln2_b"], nb, lt, alpha)
    pool_state = p.reshape(b, l, pw)[:, l - POOL_STATE:, :]
    return y, pool_state, c_new, n_new, m_new


def kernel(x_prompt, x_sample, c_prompt, c_sample, state_pool, state_mlstm_C, state_mlstm_n, state_mlstm_m, w_ada, b_ada, w_in, b_i, b_f, w_pool, pool_scale, gn_w, w_pa, w_pb, w_out, ln1_g, ln1_b, w_gate, w_up, w_down, ln2_g, ln2_b):
    depth = w_ada.shape[0]
    alpha = (2 * depth) ** 0.25
    d = x_prompt.shape[2]
    nh = b_i.shape[1]
    pw = pool_scale.shape[1]
    mw = gn_w.shape[1]
    nbp = x_prompt.shape[0]
    assert pw % COLS_MATMUL == 0 and mw % COLS_MATMUL == 0 and d % COLS_MATMUL == 0
    y_p, y_s = x_prompt, x_sample
    outs_p, outs_s = [], []
    for li in range(depth):
        wt = jnp.transpose(w_in[li])
        gates0 = pw + 4 * mw
        w = {
            "nh": nh,
            "w_in_t": wt,
            "w_if": jnp.pad(wt[gates0:gates0 + 2 * nh], ((0, LANES - 2 * nh), (0, 0))).astype(BF16),
            "gate_bias": jnp.pad(jnp.concatenate([b_i[li], b_f[li]]), (0, LANES - 2 * nh)).reshape(1, LANES),
            "gn_w": gn_w[li].reshape(1, mw),
            "w_pool": w_pool[li].astype(BF16),
            "pool_scale": pool_scale[li].reshape(1, pw),
            "w_pa": w_pa[li].astype(BF16),
            "w_pb": w_pb[li].astype(BF16),
            "w_out": w_out[li].astype(BF16),
            "ln1_g": ln1_g[li].reshape(1, d),
            "ln1_b": ln1_b[li].reshape(1, d),
            "w_gate": w_gate[li],
            "w_up": w_up[li],
            "w_down": w_down[li].astype(BF16),
            "ln2_g": ln2_g[li].reshape(1, d),
            "ln2_b": ln2_b[li].reshape(1, d),
        }
        mod = _ada(jnp.concatenate([c_prompt, c_sample], axis=0), w_ada[li], b_ada[li])
        res_p = _stream(y_p, mod[:nbp], None, 0, w, alpha)
        res_s = _stream(y_s, mod[nbp:], (state_pool[li], state_mlstm_C[li], state_mlstm_n[li], state_mlstm_m[li]),
                        PAST_LEN, w, alpha)
        y_p, y_s = res_p[0], res_s[0]
        outs_p.append(res_p[1:])
        outs_s.append(res_s[1:])
    stack = lambda outs, k: jnp.stack([o[k] for o in outs], axis=0)
    return (y_p, y_s,
            stack(outs_p, 0), stack(outs_p, 1), stack(outs_p, 2), stack(outs_p, 3),
            stack(outs_s, 0), stack(outs_s, 1), stack(outs_s, 2), stack(outs_s, 3))
```

```python
import functools

import jax
import jax.numpy as jnp
from jax import lax
from jax.experimental import pallas as pl
from jax.experimental.pallas import tpu as pltpu

F32 = jnp.float32
BF16 = jnp.bfloat16

LN_EPS = 1e-5
POOL_WINDOWS = (2, 4, 8, 16)
POOL_STATE = max(POOL_WINDOWS) - 1
POOL_HALO = 16
PAST_LEN = 1024
LANES = 128
VMEM_CAP_BYTES = 60 * 1024 * 1024
VMEM_MIN_BYTES = 32 * 1024 * 1024

ROWS_INPROJ = 1024
ROWS_MATMUL = 2048
COLS_MATMUL = 1024
COLS_FFN_UP = 512
COLS_ADA = 1024
ROWS_MIX = 256
ROWS_FFN_DOWN = 512
MLSTM_CHUNK = 256
MLSTM_SEQS = 2
MLSTM_ROWS = 128


def _vmem_limit(nbytes):
    return int(min(VMEM_CAP_BYTES, max(VMEM_MIN_BYTES, nbytes * 5 // 4)))


def _params(nbytes, ndims):
    return pltpu.CompilerParams(dimension_semantics=("arbitrary",) * ndims,
                                vmem_limit_bytes=_vmem_limit(nbytes))


def _layer_norm(x):
    mu = jnp.mean(x, axis=-1, keepdims=True)
    xc = x - mu
    var = jnp.mean(xc * xc, axis=-1, keepdims=True)
    return xc * lax.rsqrt(var + LN_EPS)


MIN_MATMUL_ROWS = 256


def _row_parts(nb, lt):
    if nb == 1 and lt % (2 * MIN_MATMUL_ROWS) == 0:
        h = lt // 2
        return [(slice(0, 1), slice(0, h), slice(0, h)), (slice(0, 1), slice(h, lt), slice(h, lt))]
    if nb % 2 == 0 and (nb // 2 * lt) % MIN_MATMUL_ROWS == 0:
        h = nb // 2
        return [(slice(0, h), slice(0, lt), slice(0, h * lt)), (slice(h, nb), slice(0, lt), slice(h * lt, nb * lt))]
    return [(slice(0, nb), slice(0, lt), slice(0, nb * lt))]


def _dot(a, b):
    return jnp.dot(a, b, preferred_element_type=F32)


def _dot_nt(a, bt):
    return lax.dot_general(a, bt, (((1,), (1,)), ((), ())), preferred_element_type=F32)


def _log_sigmoid(x):
    return jnp.minimum(x, 0.0) - jnp.log1p(jnp.exp(-jnp.abs(x)))


def _split3(x):
    hi = x.astype(BF16)
    r1 = x - hi.astype(F32)
    mid = r1.astype(BF16)
    lo = (r1 - mid.astype(F32)).astype(BF16)
    return hi, mid, lo


def _ada_kernel(c_ref, w_ref, b_ref, o_ref):
    c = c_ref[...]
    a = (c * jax.nn.sigmoid(c)).astype(BF16)
    o_ref[...] = _dot(a, w_ref[...].astype(BF16)) + b_ref[...]


def _ada(c, w_ada, b_ada):
    nb, d = c.shape
    n = w_ada.shape[1]
    tn = COLS_ADA if n % COLS_ADA == 0 else n
    nbytes = 2 * (d * tn * 4) + d * tn * 2 + 4 * nb * (d + tn) * 4
    return pl.pallas_call(
        _ada_kernel,
        grid=(n // tn,),
        in_specs=[pl.BlockSpec((nb, d), lambda j: (0, 0)),
                  pl.BlockSpec((d, tn), lambda j: (0, j)),
                  pl.BlockSpec((1, tn), lambda j: (0, j))],
        out_specs=pl.BlockSpec((nb, tn), lambda j: (0, j)),
        out_shape=jax.ShapeDtypeStruct((nb, n), F32),
        compiler_params=_params(nbytes, 1),
        name="ada",
    )(c, w_ada, b_ada.reshape(1, n))


def _inproj_kernel(x_ref, sh_ref, sc_ref, wp_ref, wif_ref, u_ref, p_ref, ifg_ref, wp_scr):
    nb, lt, d = x_ref.shape

    @pl.when((pl.program_id(0) == 0) & (pl.program_id(1) == 0))
    def _():
        wp_scr[...] = wp_ref[...].astype(BF16)

    for bs, ts, rs in _row_parts(nb, lt):
        u = _layer_norm(x_ref[bs, ts, :]) * (1.0 + sc_ref[bs, :, :]) + sh_ref[bs, :, :]
        ub = u.reshape(rs.stop - rs.start, d).astype(BF16)
        u_ref[rs, :] = ub
        p_ref[rs, :] = _dot_nt(ub, wp_scr[...])
        ifg_ref[rs, :] = _dot_nt(ub, wif_ref[...])


def _inproj(x, mod4, w_rows, pw, w_if, nb, lt):
    b, l, d = x.shape
    tm = nb * lt
    nt = l // lt
    rows = lambda bi, ti: (bi * nt + ti, 0)
    nbytes = (2 * tm * d * 4 + 2 * tm * d * 2 + 2 * tm * pw * 4 + d * pw * (4 + 2)
              + 3 * tm * d * 4 + tm * pw * 4)
    return pl.pallas_call(
        _inproj_kernel,
        grid=(b // nb, nt),
        in_specs=[pl.BlockSpec((nb, lt, d), lambda bi, ti: (bi, ti, 0)),
                  pl.BlockSpec((nb, None, 1, d), lambda bi, ti: (bi, 0, 0, 0)),
                  pl.BlockSpec((nb, None, 1, d), lambda bi, ti: (bi, 1, 0, 0)),
                  pl.BlockSpec((pw, d), lambda bi, ti: (0, 0), pipeline_mode=pl.Buffered(1)),
                  pl.BlockSpec((LANES, d), lambda bi, ti: (0, 0))],
        out_specs=[pl.BlockSpec((tm, d), rows),
                   pl.BlockSpec((tm, pw), rows),
                   pl.BlockSpec((tm, LANES), rows)],
        out_shape=[jax.ShapeDtypeStruct((b * l, d), BF16),
                   jax.ShapeDtypeStruct((b * l, pw), F32),
                   jax.ShapeDtypeStruct((b * l, LANES), F32)],
        scratch_shapes=[pltpu.VMEM((pw, d), BF16)],
        compiler_params=_params(nbytes, 2),
        name="inproj",
    )(x, mod4, mod4, w_rows, w_if)


def _matmul_kernel(a_ref, b_ref, o_ref, *w_scr):
    if w_scr:
        @pl.when(pl.program_id(1) == 0)
        def _():
            w_scr[0][...] = b_ref[...].astype(BF16)
        w = w_scr[0][...]
    else:
        w = b_ref[...]
    half = a_ref.shape[0] // 2
    if half % (2 * MIN_MATMUL_ROWS) == 0:
        o_ref[:half, :] = _dot_nt(a_ref[:half, :], w).astype(o_ref.dtype)
        o_ref[half:, :] = _dot_nt(a_ref[half:, :], w).astype(o_ref.dtype)
    else:
        o_ref[...] = _dot_nt(a_ref[...], w).astype(o_ref.dtype)


def _matmul(a, b, row0, n, tm, tn):
    m, k = a.shape
    wsize = b.dtype.itemsize
    scratch = [pltpu.VMEM((tn, k), BF16)] if b.dtype == F32 else []
    nbytes = 2 * tm * k * 2 + (2 * wsize + 2) * k * tn + 2 * tm * tn * 2 + tm * tn * 4
    if row0 % tn == 0:
        w_spec = pl.BlockSpec((tn, k), lambda j, i: (j + row0 // tn, 0))
    else:
        assert row0 % 8 == 0
        w_spec = pl.BlockSpec((pl.Element(tn), pl.Element(k)),
                              lambda j, i: (pl.multiple_of(row0 + j * tn, 8), 0))
    return pl.pallas_call(
        _matmul_kernel,
        grid=(n // tn, m // tm),
        in_specs=[pl.BlockSpec((tm, k), lambda j, i: (i, 0)), w_spec],
        out_specs=pl.BlockSpec((tm, tn), lambda j, i: (i, j)),
        out_shape=jax.ShapeDtypeStruct((m, n), BF16),
        scratch_shapes=scratch,
        compiler_params=_params(nbytes, 2),
        name="bigproj",
    )(a, b)


def _mlstm_kernel(*refs, nh, hd, zero_init):
    if zero_init:
        (q_ref, k_ref, v_ref, o_ref, ifg_ref, bias_ref, gnw_ref,
         h_ref, c_out, n_out, m_out, c_scr, m_scr, d_scr, f_scr, fb_scr) = refs
    else:
        (q_ref, k_ref, v_ref, o_ref, ifg_ref, bias_ref, gnw_ref, c0_ref, n0_ref, m0_ref,
         h_ref, c_out, n_out, m_out, c_scr, m_scr, d_scr, f_scr, fb_scr) = refs
    nseq, lc = q_ref.shape[0], q_ref.shape[1]
    nslots = nseq * nh
    ci = pl.program_id(1)

    @pl.when(ci == 0)
    def _():
        if zero_init:
            c_scr[...] = jnp.zeros_like(c_scr)
            m_scr[...] = jnp.zeros_like(m_scr)
        else:
            c_scr[:, :, 0:hd] = c0_ref[...].reshape(nslots, hd, hd)
            for j in range(nslots):
                c_scr[j, :, hd:hd + LANES] = jnp.transpose(
                    jnp.broadcast_to(n0_ref[j // nh, j % nh], (LANES, hd)))
            m_scr[...] = m0_ref[...].reshape(nslots, 1, 1)

    row_t = lax.broadcasted_iota(jnp.int32, (lc, lc), 0)
    col_s = lax.broadcasted_iota(jnp.int32, (lc, lc), 1)
    causal = col_s <= row_t
    tril = jnp.where(causal, 1.0, 0.0).astype(BF16)
    b_col, g_col, g_row = [], [], []
    for s in range(nseq):
        gates = ifg_ref[s] + bias_ref[...]
        lf = _log_sigmoid(gates)
        b_all = sum(_dot(tril, piece) for piece in _split3(lf))
        b_col.append(pltpu.roll(b_all, LANES - nh, 1))
        g_col.append(gates - b_col[s])
        g_row.append(jnp.transpose(g_col[s]))
    scale = hd ** -0.5
    scale_is_pow2 = (hd & (hd - 1)) == 0 and (hd.bit_length() - 1) % 2 == 0
    ones_v = jnp.ones((lc, LANES), BF16)
    reps = hd // LANES
    wide = lambda col: jnp.concatenate([col] * reps, axis=1)

    m_heads, decays, h_heads, c_heads = {}, {}, {}, {}

    def gate_phase(j):
        s, h = divmod(j, nh)
        m_prev = m_scr[j]
        x = jnp.where(causal, g_row[s][h:h + 1, :], -jnp.inf)
        m_run = jnp.maximum(jnp.max(x, axis=1, keepdims=True), m_prev)
        m_run_b = jnp.broadcast_to(m_run, (lc, LANES))
        if lc % LANES == 0:
            d_scr[j] = jnp.exp(x - jnp.concatenate([m_run_b] * (lc // LANES), axis=1))
        else:
            d_scr[j] = jnp.exp(x - m_run)
        m_last = m_run[lc - 1:lc, :]
        m_row = jnp.broadcast_to(b_col[s][:, h:h + 1], (lc, LANES)) + m_run_b
        f_scr[j] = jnp.exp(-m_row)
        fb_scr[j, 0] = jnp.exp(m_prev - m_run_b)
        fb_scr[j, 1] = jnp.exp(jnp.broadcast_to(g_col[s][:, h:h + 1], (lc, LANES)) - m_last)
        decays[j] = jnp.exp(m_prev - m_last)
        m_heads[j] = b_col[s][lc - 1:lc, h:h + 1] + m_last

    def matmul_phase(j):
        s, h = divmod(j, nh)
        sl = slice(h * hd, (h + 1) * hd)
        q = q_ref[s, :, sl]
        if scale_is_pow2:
            k = k_ref[s, :, sl] * scale
        else:
            k = (k_ref[s, :, sl].astype(F32) * scale).astype(BF16)
        v_aug = jnp.concatenate([v_ref[s, :, sl], ones_v], axis=1)
        c_prev = c_scr[j]

        qk = lax.dot_general(q, k, (((1,), (1,)), ((), ())), preferred_element_type=F32)
        sm = (qk * d_scr[j]).astype(BF16)
        qw = (q.astype(F32) * wide(fb_scr[j, 0])).astype(BF16)
        tot = _dot(jnp.concatenate([qw, sm], axis=1),
                   jnp.concatenate([c_prev.astype(BF16), v_aug], axis=0))
        inv = 1.0 / jnp.maximum(jnp.abs(tot[:, hd:]), f_scr[j])
        hg = tot[:, :hd] * wide(inv) * jax.nn.sigmoid(o_ref[s, :, sl].astype(F32))
        h_heads[j] = (_layer_norm(hg) * gnw_ref[:, sl]).astype(BF16)

        kw = (k.astype(F32) * wide(fb_scr[j, 1])).astype(BF16)
        c_heads[j] = decays[j] * c_prev + lax.dot_general(
            kw, v_aug, (((0,), (0,)), ((), ())), preferred_element_type=F32)

    for j in range(nslots):
        gate_phase(j)
    for j in range(nslots):
        matmul_phase(j)

    for s in range(nseq):
        h_ref[s] = jnp.concatenate([h_heads[s * nh + h] for h in range(nh)], axis=1)
    c_scr[...] = jnp.stack([c_heads[j] for j in range(nslots)], axis=0)
    m_scr[...] = jnp.stack([m_heads[j] for j in range(nslots)], axis=0)

    @pl.when(ci == pl.num_programs(1) - 1)
    def _():
        c_out[...] = c_scr[:, :, 0:hd].reshape(nseq, nh, hd, hd)
        n_out[...] = jnp.stack([jnp.transpose(c_scr[j, :, hd:hd + LANES])[0:1, :]
                                for j in range(nslots)], axis=0).reshape(nseq, nh, 1, hd)
        m_out[...] = m_scr[...].reshape(nseq, nh, 1, 1)


def _mlstm(qkvo, ifg, gate_bias, gn_w, state, b, l, lc, nh, nseq):
    mw = gn_w.shape[1]
    hd = mw // nh
    assert hd % LANES == 0 and b % nseq == 0
    nc = l // lc
    nslots = nseq * nh
    seq3 = lambda bi, ci: (bi, ci, 0)
    col = lambda j: (lambda bi, ci: (bi, ci, j))
    st4 = lambda bi, ci: (bi, 0, 0, 0)
    zero_init = state is None
    qkvo3 = qkvo.reshape(b, l, 4 * mw)
    in_specs = [pl.BlockSpec((nseq, lc, mw), col(0)), pl.BlockSpec((nseq, lc, mw), col(1)),
                pl.BlockSpec((nseq, lc, mw), col(2)), pl.BlockSpec((nseq, lc, mw), col(3)),
                pl.BlockSpec((nseq, lc, LANES), seq3),
                pl.BlockSpec((1, LANES), lambda bi, ci: (0, 0)),
                pl.BlockSpec((1, mw), lambda bi, ci: (0, 0))]
    args = [qkvo3, qkvo3, qkvo3, qkvo3, ifg.reshape(b, l, LANES), gate_bias, gn_w]
    if not zero_init:
        c0, n0, m0 = state
        in_specs += [pl.BlockSpec((nseq, nh, hd, hd), st4), pl.BlockSpec((nseq, nh, 1, hd), st4),
                     pl.BlockSpec((nseq, nh, 1, 1), st4)]
        args += [c0, n0.reshape(b, nh, 1, hd), m0.reshape(b, nh, 1, 1)]
    nbytes = (2 * 5 * nseq * lc * mw * 2 + nslots * hd * (hd + LANES) * 4
              + (2 + 2 * (not zero_init)) * nslots * hd * hd * 4
              + nslots * lc * (lc + 3 * LANES) * 4 + 16 * lc * max(lc, hd + LANES) * 4)
    h, c, n, m = pl.pallas_call(
        functools.partial(_mlstm_kernel, nh=nh, hd=hd, zero_init=zero_init),
        grid=(b // nseq, nc),
        in_specs=in_specs,
        out_specs=[pl.BlockSpec((nseq, lc, mw), seq3), pl.BlockSpec((nseq, nh, hd, hd), st4),
                   pl.BlockSpec((nseq, nh, 1, hd), st4), pl.BlockSpec((nseq, nh, 1, 1), st4)],
        out_shape=[jax.ShapeDtypeStruct((b, l, mw), BF16),
                   jax.ShapeDtypeStruct((b, nh, hd, hd), F32),
                   jax.ShapeDtypeStruct((b, nh, 1, hd), F32),
                   jax.ShapeDtypeStruct((b, nh, 1, 1), F32)],
        scratch_shapes=[pltpu.VMEM((nslots, hd, hd + LANES), F32), pltpu.VMEM((nslots, 1, 1), F32),
                        pltpu.VMEM((nslots, lc, lc), F32), pltpu.VMEM((nslots, lc, LANES), F32),
                        pltpu.VMEM((nslots, 2, lc, LANES), F32)],
        compiler_params=_params(nbytes, 2),
        name="mlstm",
    )(*args)
    return h.reshape(b * l, mw), c, n.reshape(b, nh, hd), m.reshape(b, nh)


def _mix_kernel(*refs, start_pos, alpha, has_prefix):
    if has_prefix:
        (p_ref, pre_ref, bo_ref, ga_ref, gb_ref, x_ref, g1_ref, sh2_ref, sc2_ref, wpool_ref, psc_ref,
         wpa_ref, wpb_ref, wout_ref, l1g_ref, l1b_ref, x1_ref, u2_ref, ext_scr) = refs
    else:
        (p_ref, bo_ref, ga_ref, gb_ref, x_ref, g1_ref, sh2_ref, sc2_ref, wpool_ref, psc_ref,
         wpa_ref, wpb_ref, wout_ref, l1g_ref, l1b_ref, x1_ref, u2_ref, ext_scr) = refs
    nb, lt, pw = p_ref.shape
    d = x_ref.shape[2]
    grp = pw // len(POOL_WINDOWS)
    ti = pl.program_id(1)

    @pl.when(ti == 0)
    def _():
        ext_scr[:, 0:POOL_HALO, :] = jnp.zeros((nb, POOL_HALO, pw), F32)
        if has_prefix:
            ext_scr[:, POOL_HALO - POOL_STATE:POOL_HALO, :] = pre_ref[...]

    @pl.when(ti > 0)
    def _():
        ext_scr[:, 0:POOL_HALO, :] = ext_scr[:, lt:lt + POOL_HALO, :]

    ext_scr[:, POOL_HALO:POOL_HALO + lt, :] = p_ref[...]

    pos = start_pos + ti * lt + lax.broadcasted_iota(jnp.int32, (lt, LANES), 0)
    branch = []
    for g, w in enumerate(POOL_WINDOWS):
        cs = slice(g * grp, (g + 1) * grp)
        inv = 1.0 / jnp.minimum(pos + 1, w).astype(F32)
        inv = jnp.concatenate([inv] * (grp // LANES), axis=1)
        ys = []
        for bi in range(nb):
            ext = ext_scr[bi, :, cs]
            acc = ext
            shift = 1
            while shift < w:
                acc = acc + pltpu.roll(acc, shift, 0)
                shift *= 2
            ys.append(acc[POOL_HALO:] * inv - ext[POOL_HALO:])
        y = jnp.concatenate(ys, axis=0).astype(BF16)
        branch.append((_dot(y, wpool_ref[g]) * psc_ref[:, cs]).astype(BF16))
    a_out = jnp.concatenate(branch, axis=1)

    pa = _dot(a_out, wpa_ref[...])
    pb = _dot(bo_ref[...], wpb_ref[...])
    merged = (jax.nn.sigmoid(ga_ref[...].astype(F32)) * pa
              + jax.nn.sigmoid(gb_ref[...].astype(F32)) * pb).astype(BF16)
    t = _dot(merged, wout_ref[...]).reshape(nb, lt, d)
    x1 = _layer_norm(alpha * x_ref[...] + g1_ref[...] * t) * l1g_ref[...] + l1b_ref[...]
    x1_ref[...] = x1
    u2 = _layer_norm(x1) * (1.0 + sc2_ref[...]) + sh2_ref[...]
    u2_ref[...] = u2.reshape(nb * lt, d).astype(BF16)


def _mix(p, prefix, b_out, gg, x, mod4, w_pool, pool_scale, w_pa, w_pb, w_out, ln1_g, ln1_b,
         nb, lt, start_pos, alpha):
    b, l, d = x.shape
    pw = p.shape[1]
    mw = b_out.shape[1]
    assert (pw // len(POOL_WINDOWS)) % LANES == 0
    tm = nb * lt
    nt = l // lt
    rows = lambda bi, ti: (bi * nt + ti, 0)
    seq3 = lambda bi, ti: (bi, ti, 0)
    const2 = lambda bi, ti: (0, 0)
    modk = lambda kk: (lambda bi, ti: (bi, kk, 0, 0))
    has_prefix = prefix is not None
    single = pl.Buffered(1)
    in_specs = [pl.BlockSpec((nb, lt, pw), seq3)]
    args = [p.reshape(b, l, pw)]
    if has_prefix:
        in_specs.append(pl.BlockSpec((nb, POOL_STATE, pw), lambda bi, ti: (bi, 0, 0)))
        args.append(prefix)
    in_specs += [pl.BlockSpec((tm, mw), rows),
                 pl.BlockSpec((tm, d), rows),
                 pl.BlockSpec((tm, d), lambda bi, ti: (bi * nt + ti, 1)),
                 pl.BlockSpec((nb, lt, d), seq3),
                 pl.BlockSpec((nb, None, 1, d), modk(2)),
                 pl.BlockSpec((nb, None, 1, d), modk(3)),
                 pl.BlockSpec((nb, None, 1, d), modk(4)),
                 pl.BlockSpec(w_pool.shape, lambda bi, ti: (0, 0, 0), pipeline_mode=single),
                 pl.BlockSpec((1, pw), const2),
                 pl.BlockSpec(w_pa.shape, const2, pipeline_mode=single),
                 pl.BlockSpec(w_pb.shape, const2, pipeline_mode=single),
                 pl.BlockSpec(w_out.shape, const2, pipeline_mode=single),
                 pl.BlockSpec((1, d), const2),
                 pl.BlockSpec((1, d), const2)]
    args += [b_out, gg, gg, x, mod4, mod4, mod4, w_pool, pool_scale, w_pa, w_pb, w_out, ln1_g, ln1_b]
    wbytes = 2 * (w_pool.size + w_pa.size + w_pb.size + w_out.size)
    nbytes = (wbytes + 2 * tm * (pw * 4 + mw * 2 + 2 * d * 2 + d * 4) + 2 * tm * (d * 4 + d * 2)
              + nb * (POOL_HALO + lt) * pw * 4 + 6 * tm * d * 4)
    x1, u2 = pl.pallas_call(
        functools.partial(_mix_kernel, start_pos=start_pos, alpha=alpha, has_prefix=has_prefix),
        grid=(b // nb, nt),
        in_specs=in_specs,
        out_specs=[pl.BlockSpec((nb, lt, d), seq3), pl.BlockSpec((tm, d), rows)],
        out_shape=[jax.ShapeDtypeStruct((b, l, d), F32), jax.ShapeDtypeStruct((b * l, d), BF16)],
        scratch_shapes=[pltpu.VMEM((nb, POOL_HALO + lt, pw), F32)],
        compiler_params=_params(nbytes, 2),
        name="mix",
    )(*args)
    return x1, u2


def _ffn_up_kernel(u_ref, wg_ref, wu_ref, h_ref, wg_scr, wu_scr):
    @pl.when(pl.program_id(1) == 0)
    def _():
        wg_scr[...] = wg_ref[...].astype(BF16)
        wu_scr[...] = wu_ref[...].astype(BF16)

    half = u_ref.shape[0] // 2
    parts = [slice(0, half), slice(half, 2 * half)] if half % (2 * MIN_MATMUL_ROWS) == 0 else [slice(None)]
    for rows in parts:
        u = u_ref[rows, :]
        hg = _dot(u, wg_scr[...])
        hu = _dot(u, wu_scr[...])
        h_ref[rows, :] = (hg * jax.nn.sigmoid(hg) * hu).astype(BF16)


def _ffn_up(u2, w_gate, w_up, tm, tn):
    m, d = u2.shape
    ff = w_gate.shape[1]
    nbytes = 2 * tm * d * 2 + 2 * (2 * 4 + 2) * d * tn + 2 * tm * tn * 2 + 4 * tm * tn * 4
    return pl.pallas_call(
        _ffn_up_kernel,
        grid=(ff // tn, m // tm),
        in_specs=[pl.BlockSpec((tm, d), lambda j, i: (i, 0)),
                  pl.BlockSpec((d, tn), lambda j, i: (0, j)),
                  pl.BlockSpec((d, tn), lambda j, i: (0, j))],
        out_specs=pl.BlockSpec((tm, tn), lambda j, i: (i, j)),
        out_shape=jax.ShapeDtypeStruct((m, ff), BF16),
        scratch_shapes=[pltpu.VMEM((d, tn), BF16), pltpu.VMEM((d, tn), BF16)],
        compiler_params=_params(nbytes, 2),
        name="ffn_up",
    )(u2, w_gate, w_up)


def _ffn_down_kernel(h_ref, wd_ref, x1_ref, g2_ref, l2g_ref, l2b_ref, y_ref, *, alpha):
    nb, lt, d = x1_ref.shape
    for bs, ts, rs in _row_parts(nb, lt):
        f = _dot(h_ref[rs, :], wd_ref[...]).reshape(bs.stop - bs.start, ts.stop - ts.start, d)
        z = alpha * x1_ref[bs, ts, :] + g2_ref[bs, :, :] * f
        y_ref[bs, ts, :] = _layer_norm(z) * l2g_ref[...] + l2b_ref[...]


def _ffn_down(hidden, x1, mod4, w_down, ln2_g, ln2_b, nb, lt, alpha):
    b, l, d = x1.shape
    ff = w_down.shape[0]
    tm = nb * lt
    nt = l // lt
    const2 = lambda bi, ti: (0, 0)
    seq3 = lambda bi, ti: (bi, ti, 0)
    nbytes = 2 * tm * ff * 2 + ff * d * 2 + 4 * tm * d * 4 + 4 * tm * d * 4
    return pl.pallas_call(
        functools.partial(_ffn_down_kernel, alpha=alpha),
        grid=(b // nb, nt),
        in_specs=[pl.BlockSpec((tm, ff), lambda bi, ti: (bi * nt + ti, 0)),
                  pl.BlockSpec((ff, d), const2, pipeline_mode=pl.Buffered(1)),
                  pl.BlockSpec((nb, lt, d), seq3),
                  pl.BlockSpec((nb, None, 1, d), lambda bi, ti: (bi, 5, 0, 0)),
                  pl.BlockSpec((1, d), const2),
                  pl.BlockSpec((1, d), const2)],
        out_specs=pl.BlockSpec((nb, lt, d), seq3),
        out_shape=jax.ShapeDtypeStruct((b, l, d), F32),
        compiler_params=_params(nbytes, 2),
        name="ffn_down",
    )(hidden, w_down, x1, mod4, ln2_g, ln2_b)


def _tile(total, want):
    if total <= want:
        return total
    t = want
    while total % t or t % 8:
        t -= 1
    return t


def _stream(x, mod, state, start_pos, w, alpha):
    b, l, d = x.shape
    nh = w["nh"]
    pw = w["pool_scale"].shape[1]
    mw = w["gn_w"].shape[1]
    mod4 = mod.reshape(b, 6, 1, d)

    def tiling(rows_want):
        lt = _tile(l, rows_want)
        nb = _tile(b, max(1, rows_want // lt)) if lt == l else 1
        return nb, lt

    nb, lt = tiling(ROWS_INPROJ)
    u, p, ifg = _inproj(x, mod4, w["w_in_t"], pw, w["w_if"], nb, lt)
    tm = _tile(b * l, ROWS_MATMUL)
    qkvo = _matmul(u, w["w_in_t"], pw, 4 * mw, tm, COLS_MATMUL)
    gg = _matmul(u, w["w_in_t"], pw + 4 * mw + 2 * nh, 2 * d, tm, COLS_MATMUL)

    lc = _tile(l, MLSTM_CHUNK)
    mstate = None if state is None else state[1:]
    nseq = max(MLSTM_SEQS, MLSTM_ROWS // lc)
    b_out, c_new, n_new, m_new = _mlstm(qkvo, ifg, w["gate_bias"], w["gn_w"], mstate, b, l, lc, nh,
                                        nseq if b % nseq == 0 else 1)

    nb, lt = tiling(ROWS_MIX)
    prefix = None if state is None else state[0]
    x1, u2 = _mix(p, prefix, b_out, gg, x, mod4, w["w_pool"], w["pool_scale"], w["w_pa"], w["w_pb"],
                  w["w_out"], w["ln1_g"], w["ln1_b"], nb, lt, start_pos, alpha)

    ff = w["w_gate"].shape[1]
    hidden = _ffn_up(u2, w["w_gate"], w["w_up"], tm, COLS_FFN_UP if ff % COLS_FFN_UP == 0 else ff)
    nb, lt = tiling(ROWS_FFN_DOWN)
    y = _ffn_down(hidden, x1, mod4, w["w_down"], w["ln2_g"], w["ln2_b"], nb, lt, alpha)
    pool_state = p.reshape(b, l, pw)[:, l - POOL_STATE:, :]
    return y, pool_state, c_new, n_new, m_new


def kernel(x_prompt, x_sample, c_prompt, c_sample, state_pool, state_mlstm_C, state_mlstm_n, state_mlstm_m, w_ada, b_ada, w_in, b_i, b_f, w_pool, pool_scale, gn_w, w_pa, w_pb, w_out, ln1_g, ln1_b, w_gate, w_up, w_down, ln2_g, ln2_b):
    depth = w_ada.shape[0]
    alpha = (2 * depth) ** 0.25
    d = x_prompt.shape[2]
    nh = b_i.shape[1]
    pw = pool_scale.shape[1]
    mw = gn_w.shape[1]
    nbp = x_prompt.shape[0]
    assert pw % COLS_MATMUL == 0 and mw % COLS_MATMUL == 0 and d % COLS_MATMUL == 0
    y_p, y_s = x_prompt, x_sample
    outs_p, outs_s = [], []
    for li in range(depth):
        wt = jnp.transpose(w_in[li])
        gates0 = pw + 4 * mw
        w = {
            "nh": nh,
            "w_in_t": wt,
            "w_if": jnp.pad(wt[gates0:gates0 + 2 * nh], ((0, LANES - 2 * nh), (0, 0))).astype(BF16),
            "gate_bias": jnp.pad(jnp.concatenate([b_i[li], b_f[li]]), (0, LANES - 2 * nh)).reshape(1, LANES),
            "gn_w": gn_w[li].reshape(1, mw),
            "w_pool": w_pool[li].astype(BF16),
            "pool_scale": pool_scale[li].reshape(1, pw),
            "w_pa": w_pa[li].astype(BF16),
            "w_pb": w_pb[li].astype(BF16),
            "w_out": w_out[li].astype(BF16),
            "ln1_g": ln1_g[li].reshape(1, d),
            "ln1_b": ln1_b[li].reshape(1, d),
            "w_gate": w_gate[li],
            "w_up": w_up[li],
            "w_down": w_down[li].astype(BF16),
            "ln2_g": ln2_g[li].reshape(1, d),
            "ln2_b": ln2_b[li].reshape(1, d),
        }
        mod = _ada(jnp.concatenate([c_prompt, c_sample], axis=0), w_ada[li], b_ada[li])
        res_p = _stream(y_p, mod[:nbp], None, 0, w, alpha)
        res_s = _stream(y_s, mod[nbp:], (state_pool[li], state_mlstm_C[li], state_mlstm_n[li], state_mlstm_m[li]),
                        PAST_LEN, w, alpha)
        y_p, y_s = res_p[0], res_s[0]
        outs_p.append(res_p[1:])
        outs_s.append(res_s[1:])
    stack = lambda outs, k: jnp.stack([o[k] for o in outs], axis=0)
    return (y_p, y_s,
            stack(outs_p, 0), stack(outs_p, 1), stack(outs_p, 2), stack(outs_p, 3),
            stack(outs_s, 0), stack(outs_s, 1), stack(outs_s, 2), stack(outs_s, 3))
```

```python
import functools

import jax
import jax.numpy as jnp
from jax import lax
from jax.experimental import pallas as pl
from jax.experimental.pallas import tpu as pltpu

F32 = jnp.float32
BF16 = jnp.bfloat16

LN_EPS = 1e-5
POOL_WINDOWS = (2, 4, 8, 16)
POOL_STATE = max(POOL_WINDOWS) - 1
POOL_HALO = 16
PAST_LEN = 1024
LANES = 128
VMEM_CAP_BYTES = 60 * 1024 * 1024
VMEM_MIN_BYTES = 32 * 1024 * 1024

ROWS_INPROJ = 1024
ROWS_MATMUL = 2048
COLS_MATMUL = 1024
COLS_FFN_UP = 512
COLS_ADA = 1024
ROWS_MIX = 256
ROWS_FFN_DOWN = 512
MLSTM_CHUNK = 256
MLSTM_SEQS = 2
MLSTM_ROWS = 128


def _vmem_limit(nbytes):
    return int(min(VMEM_CAP_BYTES, max(VMEM_MIN_BYTES, nbytes * 5 // 4)))


def _params(nbytes, ndims):
    return pltpu.CompilerParams(dimension_semantics=("arbitrary",) * ndims,
                                vmem_limit_bytes=_vmem_limit(nbytes))


def _layer_norm(x):
    mu = jnp.mean(x, axis=-1, keepdims=True)
    xc = x - mu
    var = jnp.mean(xc * xc, axis=-1, keepdims=True)
    return xc * lax.rsqrt(var + LN_EPS)


MIN_MATMUL_ROWS = 256


def _row_parts(nb, lt, parts=2):
    while parts > 1:
        if nb == 1 and lt % (parts * MIN_MATMUL_ROWS) == 0:
            h = lt // parts
            return [(slice(0, 1), slice(i * h, (i + 1) * h), slice(i * h, (i + 1) * h)) for i in range(parts)]
        if nb % parts == 0 and (nb // parts * lt) % MIN_MATMUL_ROWS == 0:
            h = nb // parts
            return [(slice(i * h, (i + 1) * h), slice(0, lt), slice(i * h * lt, (i + 1) * h * lt))
                    for i in range(parts)]
        parts //= 2
    return [(slice(0, nb), slice(0, lt), slice(0, nb * lt))]


def _dot(a, b):
    return jnp.dot(a, b, preferred_element_type=F32)


def _dot_nt(a, bt):
    return lax.dot_general(a, bt, (((1,), (1,)), ((), ())), preferred_element_type=F32)


def _log_sigmoid(x):
    return jnp.minimum(x, 0.0) - jnp.log1p(jnp.exp(-jnp.abs(x)))


def _split3(x):
    hi = x.astype(BF16)
    r1 = x - hi.astype(F32)
    mid = r1.astype(BF16)
    lo = (r1 - mid.astype(F32)).astype(BF16)
    return hi, mid, lo


def _ada_kernel(c_ref, w_ref, b_ref, o_ref):
    c = c_ref[...]
    a = (c * jax.nn.sigmoid(c)).astype(BF16)
    o_ref[...] = _dot(a, w_ref[...].astype(BF16)) + b_ref[...]


def _ada(c, w_ada, b_ada):
    nb, d = c.shape
    n = w_ada.shape[1]
    tn = COLS_ADA if n % COLS_ADA == 0 else n
    nbytes = 2 * (d * tn * 4) + d * tn * 2 + 4 * nb * (d + tn) * 4
    return pl.pallas_call(
        _ada_kernel,
        grid=(n // tn,),
        in_specs=[pl.BlockSpec((nb, d), lambda j: (0, 0)),
                  pl.BlockSpec((d, tn), lambda j: (0, j)),
                  pl.BlockSpec((1, tn), lambda j: (0, j))],
        out_specs=pl.BlockSpec((nb, tn), lambda j: (0, j)),
        out_shape=jax.ShapeDtypeStruct((nb, n), F32),
        compiler_params=_params(nbytes, 1),
        name="ada",
    )(c, w_ada, b_ada.reshape(1, n))


def _inproj_kernel(x_ref, sh_ref, sc_ref, wp_ref, wif_ref, u_ref, p_ref, ifg_ref, wp_scr):
    nb, lt, d = x_ref.shape

    @pl.when((pl.program_id(0) == 0) & (pl.program_id(1) == 0))
    def _():
        wp_scr[...] = wp_ref[...].astype(BF16)

    for bs, ts, rs in _row_parts(nb, lt, 4):
        u = _layer_norm(x_ref[bs, ts, :]) * (1.0 + sc_ref[bs, :, :]) + sh_ref[bs, :, :]
        ub = u.reshape(rs.stop - rs.start, d).astype(BF16)
        u_ref[rs, :] = ub
        p_ref[rs, :] = _dot_nt(ub, wp_scr[...])
        ifg_ref[rs, :] = _dot_nt(ub, wif_ref[...])


def _inproj(x, mod4, w_rows, pw, w_if, nb, lt):
    b, l, d = x.shape
    tm = nb * lt
    nt = l // lt
    rows = lambda bi, ti: (bi * nt + ti, 0)
    nbytes = (2 * tm * d * 4 + 2 * tm * d * 2 + 2 * tm * pw * 4 + d * pw * (4 + 2)
              + 3 * tm * d * 4 + tm * pw * 4)
    return pl.pallas_call(
        _inproj_kernel,
        grid=(b // nb, nt),
        in_specs=[pl.BlockSpec((nb, lt, d), lambda bi, ti: (bi, ti, 0)),
                  pl.BlockSpec((nb, None, 1, d), lambda bi, ti: (bi, 0, 0, 0)),
                  pl.BlockSpec((nb, None, 1, d), lambda bi, ti: (bi, 1, 0, 0)),
                  pl.BlockSpec((pw, d), lambda bi, ti: (0, 0), pipeline_mode=pl.Buffered(1)),
                  pl.BlockSpec((LANES, d), lambda bi, ti: (0, 0))],
        out_specs=[pl.BlockSpec((tm, d), rows),
                   pl.BlockSpec((tm, pw), rows),
                   pl.BlockSpec((tm, LANES), rows)],
        out_shape=[jax.ShapeDtypeStruct((b * l, d), BF16),
                   jax.ShapeDtypeStruct((b * l, pw), F32),
                   jax.ShapeDtypeStruct((b * l, LANES), F32)],
        scratch_shapes=[pltpu.VMEM((pw, d), BF16)],
        compiler_params=_params(nbytes, 2),
        name="inproj",
    )(x, mod4, mod4, w_rows, w_if)


def _matmul_kernel(a_ref, b_ref, o_ref, *w_scr):
    if w_scr:
        @pl.when(pl.program_id(1) == 0)
        def _():
            w_scr[0][...] = b_ref[...].astype(BF16)
        w = w_scr[0][...]
    else:
        w = b_ref[...]
    half = a_ref.shape[0] // 2
    if half % (2 * MIN_MATMUL_ROWS) == 0:
        o_ref[:half, :] = _dot_nt(a_ref[:half, :], w).astype(o_ref.dtype)
        o_ref[half:, :] = _dot_nt(a_ref[half:, :], w).astype(o_ref.dtype)
    else:
        o_ref[...] = _dot_nt(a_ref[...], w).astype(o_ref.dtype)


def _matmul(a, b, row0, n, tm, tn):
    m, k = a.shape
    wsize = b.dtype.itemsize
    scratch = [pltpu.VMEM((tn, k), BF16)] if b.dtype == F32 else []
    nbytes = 2 * tm * k * 2 + (2 * wsize + 2) * k * tn + 2 * tm * tn * 2 + tm * tn * 4
    if row0 % tn == 0:
        w_spec = pl.BlockSpec((tn, k), lambda j, i: (j + row0 // tn, 0))
    else:
        assert row0 % 8 == 0
        w_spec = pl.BlockSpec((pl.Element(tn), pl.Element(k)),
                              lambda j, i: (pl.multiple_of(row0 + j * tn, 8), 0))
    return pl.pallas_call(
        _matmul_kernel,
        grid=(n // tn, m // tm),
        in_specs=[pl.BlockSpec((tm, k), lambda j, i: (i, 0)), w_spec],
        out_specs=pl.BlockSpec((tm, tn), lambda j, i: (i, j)),
        out_shape=jax.ShapeDtypeStruct((m, n), BF16),
        scratch_shapes=scratch,
        compiler_params=_params(nbytes, 2),
        name="bigproj",
    )(a, b)


def _mlstm_kernel(*refs, nh, hd, zero_init):
    if zero_init:
        (q_ref, k_ref, v_ref, o_ref, ifg_ref, bias_ref, gnw_ref,
         h_ref, c_out, n_out, m_out, c_scr, m_scr, d_scr, f_scr, fb_scr) = refs
    else:
        (q_ref, k_ref, v_ref, o_ref, ifg_ref, bias_ref, gnw_ref, c0_ref, n0_ref, m0_ref,
         h_ref, c_out, n_out, m_out, c_scr, m_scr, d_scr, f_scr, fb_scr) = refs
    nseq, lc = q_ref.shape[0], q_ref.shape[1]
    nslots = nseq * nh
    ci = pl.program_id(1)

    @pl.when(ci == 0)
    def _():
        if zero_init:
            c_scr[...] = jnp.zeros_like(c_scr)
            m_scr[...] = jnp.zeros_like(m_scr)
        else:
            c_scr[:, :, 0:hd] = c0_ref[...].reshape(nslots, hd, hd)
            for j in range(nslots):
                c_scr[j, :, hd:hd + LANES] = jnp.transpose(
                    jnp.broadcast_to(n0_ref[j // nh, j % nh], (LANES, hd)))
            m_scr[...] = m0_ref[...].reshape(nslots, 1, 1)

    row_t = lax.broadcasted_iota(jnp.int32, (lc, lc), 0)
    col_s = lax.broadcasted_iota(jnp.int32, (lc, lc), 1)
    causal = col_s <= row_t
    tril = jnp.where(causal, 1.0, 0.0).astype(BF16)
    b_col, g_col, g_row = [], [], []
    for s in range(nseq):
        gates = ifg_ref[s] + bias_ref[...]
        lf = _log_sigmoid(gates)
        b_all = sum(_dot(tril, piece) for piece in _split3(lf))
        b_col.append(pltpu.roll(b_all, LANES - nh, 1))
        g_col.append(gates - b_col[s])
        g_row.append(jnp.transpose(g_col[s]))
    scale = hd ** -0.5
    scale_is_pow2 = (hd & (hd - 1)) == 0 and (hd.bit_length() - 1) % 2 == 0
    ones_v = jnp.ones((lc, LANES), BF16)
    reps = hd // LANES
    wide = lambda col: jnp.concatenate([col] * reps, axis=1)

    m_heads, decays, h_heads, c_heads = {}, {}, {}, {}

    def gate_phase(j):
        s, h = divmod(j, nh)
        m_prev = m_scr[j]
        x = jnp.where(causal, g_row[s][h:h + 1, :], -jnp.inf)
        m_run = jnp.maximum(jnp.max(x, axis=1, keepdims=True), m_prev)
        m_run_b = jnp.broadcast_to(m_run, (lc, LANES))
        if lc % LANES == 0:
            d_scr[j] = jnp.exp(x - jnp.concatenate([m_run_b] * (lc // LANES), axis=1))
        else:
            d_scr[j] = jnp.exp(x - m_run)
        m_last = m_run[lc - 1:lc, :]
        m_row = jnp.broadcast_to(b_col[s][:, h:h + 1], (lc, LANES)) + m_run_b
        f_scr[j] = jnp.exp(-m_row)
        fb_scr[j, 0] = jnp.exp(m_prev - m_run_b)
        fb_scr[j, 1] = jnp.exp(jnp.broadcast_to(g_col[s][:, h:h + 1], (lc, LANES)) - m_last)
        decays[j] = jnp.exp(m_prev - m_last)
        m_heads[j] = b_col[s][lc - 1:lc, h:h + 1] + m_last

    def matmul_phase(j):
        s, h = divmod(j, nh)
        sl = slice(h * hd, (h + 1) * hd)
        q = q_ref[s, :, sl]
        if scale_is_pow2:
            k = k_ref[s, :, sl] * scale
        else:
            k = (k_ref[s, :, sl].astype(F32) * scale).astype(BF16)
        v_aug = jnp.concatenate([v_ref[s, :, sl], ones_v], axis=1)
        c_prev = c_scr[j]

        qk = lax.dot_general(q, k, (((1,), (1,)), ((), ())), preferred_element_type=F32)
        sm = (qk * d_scr[j]).astype(BF16)
        qw = (q.astype(F32) * wide(fb_scr[j, 0])).astype(BF16)
        tot = _dot(jnp.concatenate([qw, sm], axis=1),
                   jnp.concatenate([c_prev.astype(BF16), v_aug], axis=0))
        inv = 1.0 / jnp.maximum(jnp.abs(tot[:, hd:]), f_scr[j])
        hg = tot[:, :hd] * wide(inv) * jax.nn.sigmoid(o_ref[s, :, sl].astype(F32))
        h_heads[j] = (_layer_norm(hg) * gnw_ref[:, sl]).astype(BF16)

        kw = (k.astype(F32) * wide(fb_scr[j, 1])).astype(BF16)
        c_heads[j] = decays[j] * c_prev + lax.dot_general(
            kw, v_aug, (((0,), (0,)), ((), ())), preferred_element_type=F32)

    for j in range(nslots):
        gate_phase(j)
    for j in range(nslots):
        matmul_phase(j)

    for s in range(nseq):
        h_ref[s] = jnp.concatenate([h_heads[s * nh + h] for h in range(nh)], axis=1)
    c_scr[...] = jnp.stack([c_heads[j] for j in range(nslots)], axis=0)
    m_scr[...] = jnp.stack([m_heads[j] for j in range(nslots)], axis=0)

    @pl.when(ci == pl.num_programs(1) - 1)
    def _():
        c_out[...] = c_scr[:, :, 0:hd].reshape(nseq, nh, hd, hd)
        n_out[...] = jnp.stack([jnp.transpose(c_scr[j, :, hd:hd + LANES])[0:1, :]
                                for j in range(nslots)], axis=0).reshape(nseq, nh, 1, hd)
        m_out[...] = m_scr[...].reshape(nseq, nh, 1, 1)


def _mlstm(qkvo, ifg, gate_bias, gn_w, state, b, l, lc, nh, nseq):
    mw = gn_w.shape[1]
    hd = mw // nh
    assert hd % LANES == 0 and b % nseq == 0
    nc = l // lc
    nslots = nseq * nh
    seq3 = lambda bi, ci: (bi, ci, 0)
    col = lambda j: (lambda bi, ci: (bi, ci, j))
    st4 = lambda bi, ci: (bi, 0, 0, 0)
    zero_init = state is None
    qkvo3 = qkvo.reshape(b, l, 4 * mw)
    in_specs = [pl.BlockSpec((nseq, lc, mw), col(0)), pl.BlockSpec((nseq, lc, mw), col(1)),
                pl.BlockSpec((nseq, lc, mw), col(2)), pl.BlockSpec((nseq, lc, mw), col(3)),
                pl.BlockSpec((nseq, lc, LANES), seq3),
                pl.BlockSpec((1, LANES), lambda bi, ci: (0, 0)),
                pl.BlockSpec((1, mw), lambda bi, ci: (0, 0))]
    args = [qkvo3, qkvo3, qkvo3, qkvo3, ifg.reshape(b, l, LANES), gate_bias, gn_w]
    if not zero_init:
        c0, n0, m0 = state
        in_specs += [pl.BlockSpec((nseq, nh, hd, hd), st4), pl.BlockSpec((nseq, nh, 1, hd), st4),
                     pl.BlockSpec((nseq, nh, 1, 1), st4)]
        args += [c0, n0.reshape(b, nh, 1, hd), m0.reshape(b, nh, 1, 1)]
    nbytes = (2 * 5 * nseq * lc * mw * 2 + nslots * hd * (hd + LANES) * 4
              + (2 + 2 * (not zero_init)) * nslots * hd * hd * 4
              + nslots * lc * (lc + 3 * LANES) * 4 + 16 * lc * max(lc, hd + LANES) * 4)
    h, c, n, m = pl.pallas_call(
        functools.partial(_mlstm_kernel, nh=nh, hd=hd, zero_init=zero_init),
        grid=(b // nseq, nc),
        in_specs=in_specs,
        out_specs=[pl.BlockSpec((nseq, lc, mw), seq3), pl.BlockSpec((nseq, nh, hd, hd), st4),
                   pl.BlockSpec((nseq, nh, 1, hd), st4), pl.BlockSpec((nseq, nh, 1, 1), st4)],
        out_shape=[jax.ShapeDtypeStruct((b, l, mw), BF16),
                   jax.ShapeDtypeStruct((b, nh, hd, hd), F32),
                   jax.ShapeDtypeStruct((b, nh, 1, hd), F32),
                   jax.ShapeDtypeStruct((b, nh, 1, 1), F32)],
        scratch_shapes=[pltpu.VMEM((nslots, hd, hd + LANES), F32), pltpu.VMEM((nslots, 1, 1), F32),
                        pltpu.VMEM((nslots, lc, lc), F32), pltpu.VMEM((nslots, lc, LANES), F32),
                        pltpu.VMEM((nslots, 2, lc, LANES), F32)],
        compiler_params=_params(nbytes, 2),
        name="mlstm",
    )(*args)
    return h.reshape(b * l, mw), c, n.reshape(b, nh, hd), m.reshape(b, nh)


def _mix_kernel(*refs, start_pos, alpha, has_prefix):
    if has_prefix:
        (p_ref, pre_ref, bo_ref, ga_ref, gb_ref, x_ref, g1_ref, sh2_ref, sc2_ref, wpool_ref, psc_ref,
         wpa_ref, wpb_ref, wout_ref, l1g_ref, l1b_ref, x1_ref, u2_ref, ext_scr) = refs
    else:
        (p_ref, bo_ref, ga_ref, gb_ref, x_ref, g1_ref, sh2_ref, sc2_ref, wpool_ref, psc_ref,
         wpa_ref, wpb_ref, wout_ref, l1g_ref, l1b_ref, x1_ref, u2_ref, ext_scr) = refs
    nb, lt, pw = p_ref.shape
    d = x_ref.shape[2]
    grp = pw // len(POOL_WINDOWS)
    ti = pl.program_id(1)

    @pl.when(ti == 0)
    def _():
        ext_scr[:, 0:POOL_HALO, :] = jnp.zeros((nb, POOL_HALO, pw), F32)
        if has_prefix:
            ext_scr[:, POOL_HALO - POOL_STATE:POOL_HALO, :] = pre_ref[...]

    @pl.when(ti > 0)
    def _():
        ext_scr[:, 0:POOL_HALO, :] = ext_scr[:, lt:lt + POOL_HALO, :]

    ext_scr[:, POOL_HALO:POOL_HALO + lt, :] = p_ref[...]

    pos = start_pos + ti * lt + lax.broadcasted_iota(jnp.int32, (lt, LANES), 0)
    branch = []
    for g, w in enumerate(POOL_WINDOWS):
        cs = slice(g * grp, (g + 1) * grp)
        inv = 1.0 / jnp.minimum(pos + 1, w).astype(F32)
        inv = jnp.concatenate([inv] * (grp // LANES), axis=1)
        ys = []
        for bi in range(nb):
            ext = ext_scr[bi, :, cs]
            acc = ext
            shift = 1
            while shift < w:
                acc = acc + pltpu.roll(acc, shift, 0)
                shift *= 2
            ys.append(acc[POOL_HALO:] * inv - ext[POOL_HALO:])
        y = jnp.concatenate(ys, axis=0).astype(BF16)
        branch.append((_dot(y, wpool_ref[g]) * psc_ref[:, cs]).astype(BF16))
    a_out = jnp.concatenate(branch, axis=1)

    pa = _dot(a_out, wpa_ref[...])
    pb = _dot(bo_ref[...], wpb_ref[...])
    merged = (jax.nn.sigmoid(ga_ref[...].astype(F32)) * pa
              + jax.nn.sigmoid(gb_ref[...].astype(F32)) * pb).astype(BF16)
    t = _dot(merged, wout_ref[...]).reshape(nb, lt, d)
    x1 = _layer_norm(alpha * x_ref[...] + g1_ref[...] * t) * l1g_ref[...] + l1b_ref[...]
    x1_ref[...] = x1
    u2 = _layer_norm(x1) * (1.0 + sc2_ref[...]) + sh2_ref[...]
    u2_ref[...] = u2.reshape(nb * lt, d).astype(BF16)


def _mix(p, prefix, b_out, gg, x, mod4, w_pool, pool_scale, w_pa, w_pb, w_out, ln1_g, ln1_b,
         nb, lt, start_pos, alpha):
    b, l, d = x.shape
    pw = p.shape[1]
    mw = b_out.shape[1]
    assert (pw // len(POOL_WINDOWS)) % LANES == 0
    tm = nb * lt
    nt = l // lt
    rows = lambda bi, ti: (bi * nt + ti, 0)
    seq3 = lambda bi, ti: (bi, ti, 0)
    const2 = lambda bi, ti: (0, 0)
    modk = lambda kk: (lambda bi, ti: (bi, kk, 0, 0))
    has_prefix = prefix is not None
    single = pl.Buffered(1)
    in_specs = [pl.BlockSpec((nb, lt, pw), seq3)]
    args = [p.reshape(b, l, pw)]
    if has_prefix:
        in_specs.append(pl.BlockSpec((nb, POOL_STATE, pw), lambda bi, ti: (bi, 0, 0)))
        args.append(prefix)
    in_specs += [pl.BlockSpec((tm, mw), rows),
                 pl.BlockSpec((tm, d), rows),
                 pl.BlockSpec((tm, d), lambda bi, ti: (bi * nt + ti, 1)),
                 pl.BlockSpec((nb, lt, d), seq3),
                 pl.BlockSpec((nb, None, 1, d), modk(2)),
                 pl.BlockSpec((nb, None, 1, d), modk(3)),
                 pl.BlockSpec((nb, None, 1, d), modk(4)),
                 pl.BlockSpec(w_pool.shape, lambda bi, ti: (0, 0, 0), pipeline_mode=single),
                 pl.BlockSpec((1, pw), const2),
                 pl.BlockSpec(w_pa.shape, const2, pipeline_mode=single),
                 pl.BlockSpec(w_pb.shape, const2, pipeline_mode=single),
                 pl.BlockSpec(w_out.shape, const2, pipeline_mode=single),
                 pl.BlockSpec((1, d), const2),
                 pl.BlockSpec((1, d), const2)]
    args += [b_out, gg, gg, x, mod4, mod4, mod4, w_pool, pool_scale, w_pa, w_pb, w_out, ln1_g, ln1_b]
    wbytes = 2 * (w_pool.size + w_pa.size + w_pb.size + w_out.size)
    nbytes = (wbytes + 2 * tm * (pw * 4 + mw * 2 + 2 * d * 2 + d * 4) + 2 * tm * (d * 4 + d * 2)
              + nb * (POOL_HALO + lt) * pw * 4 + 6 * tm * d * 4)
    x1, u2 = pl.pallas_call(
        functools.partial(_mix_kernel, start_pos=start_pos, alpha=alpha, has_prefix=has_prefix),
        grid=(b // nb, nt),
        in_specs=in_specs,
        out_specs=[pl.BlockSpec((nb, lt, d), seq3), pl.BlockSpec((tm, d), rows)],
        out_shape=[jax.ShapeDtypeStruct((b, l, d), F32), jax.ShapeDtypeStruct((b * l, d), BF16)],
        scratch_shapes=[pltpu.VMEM((nb, POOL_HALO + lt, pw), F32)],
        compiler_params=_params(nbytes, 2),
        name="mix",
    )(*args)
    return x1, u2


def _ffn_up_kernel(u_ref, wg_ref, wu_ref, h_ref, wg_scr, wu_scr):
    @pl.when(pl.program_id(1) == 0)
    def _():
        wg_scr[...] = wg_ref[...].astype(BF16)
        wu_scr[...] = wu_ref[...].astype(BF16)

    half = u_ref.shape[0] // 2
    parts = [slice(0, half), slice(half, 2 * half)] if half % (2 * MIN_MATMUL_ROWS) == 0 else [slice(None)]
    for rows in parts:
        u = u_ref[rows, :]
        hg = _dot(u, wg_scr[...])
        hu = _dot(u, wu_scr[...])
        h_ref[rows, :] = (hg * jax.nn.sigmoid(hg) * hu).astype(BF16)


def _ffn_up(u2, w_gate, w_up, tm, tn):
    m, d = u2.shape
    ff = w_gate.shape[1]
    nbytes = 2 * tm * d * 2 + 2 * (2 * 4 + 2) * d * tn + 2 * tm * tn * 2 + 4 * tm * tn * 4
    return pl.pallas_call(
        _ffn_up_kernel,
        grid=(ff // tn, m // tm),
        in_specs=[pl.BlockSpec((tm, d), lambda j, i: (i, 0)),
                  pl.BlockSpec((d, tn), lambda j, i: (0, j)),
                  pl.BlockSpec((d, tn), lambda j, i: (0, j))],
        out_specs=pl.BlockSpec((tm, tn), lambda j, i: (i, j)),
        out_shape=jax.ShapeDtypeStruct((m, ff), BF16),
        scratch_shapes=[pltpu.VMEM((d, tn), BF16), pltpu.VMEM((d, tn), BF16)],
        compiler_params=_params(nbytes, 2),
        name="ffn_up",
    )(u2, w_gate, w_up)


def _ffn_down_kernel(h_ref, wd_ref, x1_ref, g2_ref, l2g_ref, l2b_ref, y_ref, *, alpha):
    nb, lt, d = x1_ref.shape
    for bs, ts, rs in _row_parts(nb, lt):
        f = _dot(h_ref[rs, :], wd_ref[...]).reshape(bs.stop - bs.start, ts.stop - ts.start, d)
        z = alpha * x1_ref[bs, ts, :] + g2_ref[bs, :, :] * f
        y_ref[bs, ts, :] = _layer_norm(z) * l2g_ref[...] + l2b_ref[...]


def _ffn_down(hidden, x1, mod4, w_down, ln2_g, ln2_b, nb, lt, alpha):
    b, l, d = x1.shape
    ff = w_down.shape[0]
    tm = nb * lt
    nt = l // lt
    const2 = lambda bi, ti: (0, 0)
    seq3 = lambda bi, ti: (bi, ti, 0)
    nbytes = 2 * tm * ff * 2 + ff * d * 2 + 4 * tm * d * 4 + 4 * tm * d * 4
    return pl.pallas_call(
        functools.partial(_ffn_down_kernel, alpha=alpha),
        grid=(b // nb, nt),
        in_specs=[pl.BlockSpec((tm, ff), lambda bi, ti: (bi * nt + ti, 0)),
                  pl.BlockSpec((ff, d), const2, pipeline_mode=pl.Buffered(1)),
                  pl.BlockSpec((nb, lt, d), seq3),
                  pl.BlockSpec((nb, None, 1, d), lambda bi, ti: (bi, 5, 0, 0)),
                  pl.BlockSpec((1, d), const2),
                  pl.BlockSpec((1, d), const2)],
        out_specs=pl.BlockSpec((nb, lt, d), seq3),
        out_shape=jax.ShapeDtypeStruct((b, l, d), F32),
        compiler_params=_params(nbytes, 2),
        name="ffn_down",
    )(hidden, w_down, x1, mod4, ln2_g, ln2_b)


def _tile(total, want):
    if total <= want:
        return total
    t = want
    while total % t or t % 8:
        t -= 1
    return t


def _stream(x, mod, state, start_pos, w, alpha):
    b, l, d = x.shape
    nh = w["nh"]
    pw = w["pool_scale"].shape[1]
    mw = w["gn_w"].shape[1]
    mod4 = mod.reshape(b, 6, 1, d)

    def tiling(rows_want):
        lt = _tile(l, rows_want)
        nb = _tile(b, max(1, rows_want // lt)) if lt == l else 1
        return nb, lt

    nb, lt = tiling(ROWS_INPROJ)
    u, p, ifg = _inproj(x, mod4, w["w_in_t"], pw, w["w_if"], nb, lt)
    tm = _tile(b * l, ROWS_MATMUL)
    qkvo = _matmul(u, w["w_in_t"], pw, 4 * mw, tm, COLS_MATMUL)
    gg = _matmul(u, w["w_in_t"], pw + 4 * mw + 2 * nh, 2 * d, tm, COLS_MATMUL)

    lc = _tile(l, MLSTM_CHUNK)
    mstate = None if state is None else state[1:]
    nseq = max(MLSTM_SEQS, MLSTM_ROWS // lc)
    b_out, c_new, n_new, m_new = _mlstm(qkvo, ifg, w["gate_bias"], w["gn_w"], mstate, b, l, lc, nh,
                                        nseq if b % nseq == 0 else 1)

    nb, lt = tiling(ROWS_MIX)
    prefix = None if state is None else state[0]
    x1, u2 = _mix(p, prefix, b_out, gg, x, mod4, w["w_pool"], w["pool_scale"], w["w_pa"], w["w_pb"],
                  w["w_out"], w["ln1_g"], w["ln1_b"], nb, lt, start_pos, alpha)

    ff = w["w_gate"].shape[1]
    hidden = _ffn_up(u2, w["w_gate"], w["w_up"], tm, COLS_FFN_UP if ff % COLS_FFN_UP == 0 else ff)
    nb, lt = tiling(ROWS_FFN_DOWN)
    y = _ffn_down(hidden, x1, mod4, w["w_down"], w["ln2_g"], w["ln2_b"], nb, lt, alpha)
    pool_state = p.reshape(b, l, pw)[:, l - POOL_STATE:, :]
    return y, pool_state, c_new, n_new, m_new


def kernel(x_prompt, x_sample, c_prompt, c_sample, state_pool, state_mlstm_C, state_mlstm_n, state_mlstm_m, w_ada, b_ada, w_in, b_i, b_f, w_pool, pool_scale, gn_w, w_pa, w_pb, w_out, ln1_g, ln1_b, w_gate, w_up, w_down, ln2_g, ln2_b):
    depth = w_ada.shape[0]
    alpha = (2 * depth) ** 0.25
    d = x_prompt.shape[2]
    nh = b_i.shape[1]
    pw = pool_scale.shape[1]
    mw = gn_w.shape[1]
    nbp = x_prompt.shape[0]
    assert pw % COLS_MATMUL == 0 and mw % COLS_MATMUL == 0 and d % COLS_MATMUL == 0
    y_p, y_s = x_prompt, x_sample
    outs_p, outs_s = [], []
    for li in range(depth):
        wt = jnp.transpose(w_in[li])
        gates0 = pw + 4 * mw
        w = {
            "nh": nh,
            "w_in_t": wt,
            "w_if": jnp.pad(wt[gates0:gates0 + 2 * nh], ((0, LANES - 2 * nh), (0, 0))).astype(BF16),
            "gate_bias": jnp.pad(jnp.concatenate([b_i[li], b_f[li]]), (0, LANES - 2 * nh)).reshape(1, LANES),
            "gn_w": gn_w[li].reshape(1, mw),
            "w_pool": w_pool[li].astype(BF16),
            "pool_scale": pool_scale[li].reshape(1, pw),
            "w_pa": w_pa[li].astype(BF16),
            "w_pb": w_pb[li].astype(BF16),
            "w_out": w_out[li].astype(BF16),
            "ln1_g": ln1_g[li].reshape(1, d),
            "ln1_b": ln1_b[li].reshape(1, d),
            "w_gate": w_gate[li],
            "w_up": w_up[li],
            "w_down": w_down[li].astype(BF16),
            "ln2_g": ln2_g[li].reshape(1, d),
            "ln2_b": ln2_b[li].reshape(1, d),
        }
        mod = _ada(jnp.concatenate([c_prompt, c_sample], axis=0), w_ada[li], b_ada[li])
        res_p = _stream(y_p, mod[:nbp], None, 0, w, alpha)
        res_s = _stream(y_s, mod[nbp:], (state_pool[li], state_mlstm_C[li], state_mlstm_n[li], state_mlstm_m[li]),
                        PAST_LEN, w, alpha)
        y_p, y_s = res_p[0], res_s[0]
        outs_p.append(res_p[1:])
        outs_s.append(res_s[1:])
    stack = lambda outs, k: jnp.stack([o[k] for o in outs], axis=0)
    return (y_p, y_s,
            stack(outs_p, 0), stack(outs_p, 1), stack(outs_p, 2), stack(outs_p, 3),
            stack(outs_s, 0), stack(outs_s, 1), stack(outs_s, 2), stack(outs_s, 3))
```

```python
import functools

import jax
import jax.numpy as jnp
from jax import lax
from jax.experimental import pallas as pl
from jax.experimental.pallas import tpu as pltpu

F32 = jnp.float32
BF16 = jnp.bfloat16

LN_EPS = 1e-5
POOL_WINDOWS = (2, 4, 8, 16)
POOL_STATE = max(POOL_WINDOWS) - 1
POOL_HALO = 16
PAST_LEN = 1024
LANES = 128
VMEM_CAP_BYTES = 60 * 1024 * 1024
VMEM_MIN_BYTES = 32 * 1024 * 1024

ROWS_INPROJ = 1024
ROWS_MATMUL = 2048
COLS_MATMUL = 1024
COLS_FFN_UP = 512
COLS_ADA = 1024
ROWS_MIX = 256
MIX_PART_ROWS = 128
ROWS_FFN_DOWN = 512
MLSTM_CHUNK = 256
MLSTM_SEQS = 2
MLSTM_ROWS = 128


def _vmem_limit(nbytes):
    return int(min(VMEM_CAP_BYTES, max(VMEM_MIN_BYTES, nbytes * 5 // 4)))


def _params(nbytes, ndims):
    return pltpu.CompilerParams(dimension_semantics=("arbitrary",) * ndims,
                                vmem_limit_bytes=_vmem_limit(nbytes))


def _layer_norm(x):
    mu = jnp.mean(x, axis=-1, keepdims=True)
    xc = x - mu
    var = jnp.mean(xc * xc, axis=-1, keepdims=True)
    return xc * lax.rsqrt(var + LN_EPS)


MIN_MATMUL_ROWS = 256


def _row_parts(nb, lt, parts=2, min_rows=None):
    min_rows = MIN_MATMUL_ROWS if min_rows is None else min_rows
    while parts > 1:
        if nb == 1 and lt % (parts * min_rows) == 0:
            h = lt // parts
            return [(slice(0, 1), slice(i * h, (i + 1) * h), slice(i * h, (i + 1) * h)) for i in range(parts)]
        if nb % parts == 0 and (nb // parts * lt) % min_rows == 0:
            h = nb // parts
            return [(slice(i * h, (i + 1) * h), slice(0, lt), slice(i * h * lt, (i + 1) * h * lt))
                    for i in range(parts)]
        parts //= 2
    return [(slice(0, nb), slice(0, lt), slice(0, nb * lt))]


def _dot(a, b):
    return jnp.dot(a, b, preferred_element_type=F32)


def _dot_nt(a, bt):
    return lax.dot_general(a, bt, (((1,), (1,)), ((), ())), preferred_element_type=F32)


def _log_sigmoid(x):
    return jnp.minimum(x, 0.0) - jnp.log1p(jnp.exp(-jnp.abs(x)))


def _split3(x):
    hi = x.astype(BF16)
    r1 = x - hi.astype(F32)
    mid = r1.astype(BF16)
    lo = (r1 - mid.astype(F32)).astype(BF16)
    return hi, mid, lo


def _ada_kernel(c_ref, w_ref, b_ref, o_ref):
    c = c_ref[...]
    a = (c * jax.nn.sigmoid(c)).astype(BF16)
    o_ref[...] = _dot(a, w_ref[...].astype(BF16)) + b_ref[...]


def _ada(c, w_ada, b_ada):
    nb, d = c.shape
    n = w_ada.shape[1]
    tn = COLS_ADA if n % COLS_ADA == 0 else n
    nbytes = 2 * (d * tn * 4) + d * tn * 2 + 4 * nb * (d + tn) * 4
    return pl.pallas_call(
        _ada_kernel,
        grid=(n // tn,),
        in_specs=[pl.BlockSpec((nb, d), lambda j: (0, 0)),
                  pl.BlockSpec((d, tn), lambda j: (0, j)),
                  pl.BlockSpec((1, tn), lambda j: (0, j))],
        out_specs=pl.BlockSpec((nb, tn), lambda j: (0, j)),
        out_shape=jax.ShapeDtypeStruct((nb, n), F32),
        compiler_params=_params(nbytes, 1),
        name="ada",
    )(c, w_ada, b_ada.reshape(1, n))


def _inproj_kernel(x_ref, sh_ref, sc_ref, wp_ref, wif_ref, u_ref, p_ref, ifg_ref, wp_scr):
    nb, lt, d = x_ref.shape

    @pl.when((pl.program_id(0) == 0) & (pl.program_id(1) == 0))
    def _():
        wp_scr[...] = wp_ref[...].astype(BF16)

    for bs, ts, rs in _row_parts(nb, lt, 4):
        u = _layer_norm(x_ref[bs, ts, :]) * (1.0 + sc_ref[bs, :, :]) + sh_ref[bs, :, :]
        ub = u.reshape(rs.stop - rs.start, d).astype(BF16)
        u_ref[rs, :] = ub
        p_ref[rs, :] = _dot_nt(ub, wp_scr[...])
        ifg_ref[rs, :] = _dot_nt(ub, wif_ref[...])


def _inproj(x, mod4, w_rows, pw, w_if, nb, lt):
    b, l, d = x.shape
    tm = nb * lt
    nt = l // lt
    rows = lambda bi, ti: (bi * nt + ti, 0)
    nbytes = (2 * tm * d * 4 + 2 * tm * d * 2 + 2 * tm * pw * 4 + d * pw * (4 + 2)
              + 3 * tm * d * 4 + tm * pw * 4)
    return pl.pallas_call(
        _inproj_kernel,
        grid=(b // nb, nt),
        in_specs=[pl.BlockSpec((nb, lt, d), lambda bi, ti: (bi, ti, 0)),
                  pl.BlockSpec((nb, None, 1, d), lambda bi, ti: (bi, 0, 0, 0)),
                  pl.BlockSpec((nb, None, 1, d), lambda bi, ti: (bi, 1, 0, 0)),
                  pl.BlockSpec((pw, d), lambda bi, ti: (0, 0), pipeline_mode=pl.Buffered(1)),
                  pl.BlockSpec((LANES, d), lambda bi, ti: (0, 0))],
        out_specs=[pl.BlockSpec((tm, d), rows),
                   pl.BlockSpec((tm, pw), rows),
                   pl.BlockSpec((tm, LANES), rows)],
        out_shape=[jax.ShapeDtypeStruct((b * l, d), BF16),
                   jax.ShapeDtypeStruct((b * l, pw), F32),
                   jax.ShapeDtypeStruct((b * l, LANES), F32)],
        scratch_shapes=[pltpu.VMEM((pw, d), BF16)],
        compiler_params=_params(nbytes, 2),
        name="inproj",
    )(x, mod4, mod4, w_rows, w_if)


def _matmul_kernel(a_ref, b_ref, o_ref, *w_scr):
    if w_scr:
        @pl.when(pl.program_id(1) == 0)
        def _():
            w_scr[0][...] = b_ref[...].astype(BF16)
        w = w_scr[0][...]
    else:
        w = b_ref[...]
    half = a_ref.shape[0] // 2
    if half % (2 * MIN_MATMUL_ROWS) == 0:
        o_ref[:half, :] = _dot_nt(a_ref[:half, :], w).astype(o_ref.dtype)
        o_ref[half:, :] = _dot_nt(a_ref[half:, :], w).astype(o_ref.dtype)
    else:
        o_ref[...] = _dot_nt(a_ref[...], w).astype(o_ref.dtype)


def _matmul(a, b, row0, n, tm, tn):
    m, k = a.shape
    wsize = b.dtype.itemsize
    scratch = [pltpu.VMEM((tn, k), BF16)] if b.dtype == F32 else []
    nbytes = 2 * tm * k * 2 + (2 * wsize + 2) * k * tn + 2 * tm * tn * 2 + tm * tn * 4
    if row0 % tn == 0:
        w_spec = pl.BlockSpec((tn, k), lambda j, i: (j + row0 // tn, 0))
    else:
        assert row0 % 8 == 0
        w_spec = pl.BlockSpec((pl.Element(tn), pl.Element(k)),
                              lambda j, i: (pl.multiple_of(row0 + j * tn, 8), 0))
    return pl.pallas_call(
        _matmul_kernel,
        grid=(n // tn, m // tm),
        in_specs=[pl.BlockSpec((tm, k), lambda j, i: (i, 0)), w_spec],
        out_specs=pl.BlockSpec((tm, tn), lambda j, i: (i, j)),
        out_shape=jax.ShapeDtypeStruct((m, n), BF16),
        scratch_shapes=scratch,
        compiler_params=_params(nbytes, 2),
        name="bigproj",
    )(a, b)


def _mlstm_kernel(*refs, nh, hd, zero_init):
    if zero_init:
        (q_ref, k_ref, v_ref, o_ref, ifg_ref, bias_ref, gnw_ref,
         h_ref, c_out, n_out, m_out, c_scr, m_scr, d_scr, f_scr, fb_scr) = refs
    else:
        (q_ref, k_ref, v_ref, o_ref, ifg_ref, bias_ref, gnw_ref, c0_ref, n0_ref, m0_ref,
         h_ref, c_out, n_out, m_out, c_scr, m_scr, d_scr, f_scr, fb_scr) = refs
    nseq, lc = q_ref.shape[0], q_ref.shape[1]
    nslots = nseq * nh
    ci = pl.program_id(1)

    @pl.when(ci == 0)
    def _():
        if zero_init:
            c_scr[...] = jnp.zeros_like(c_scr)
            m_scr[...] = jnp.zeros_like(m_scr)
        else:
            c_scr[:, :, 0:hd] = c0_ref[...].reshape(nslots, hd, hd)
            for j in range(nslots):
                c_scr[j, :, hd:hd + LANES] = jnp.transpose(
                    jnp.broadcast_to(n0_ref[j // nh, j % nh], (LANES, hd)))
            m_scr[...] = m0_ref[...].reshape(nslots, 1, 1)

    row_t = lax.broadcasted_iota(jnp.int32, (lc, lc), 0)
    col_s = lax.broadcasted_iota(jnp.int32, (lc, lc), 1)
    causal = col_s <= row_t
    tril = jnp.where(causal, 1.0, 0.0).astype(BF16)
    b_col, g_col, g_row = [], [], []
    for s in range(nseq):
        gates = ifg_ref[s] + bias_ref[...]
        lf = _log_sigmoid(gates)
        b_all = sum(_dot(tril, piece) for piece in _split3(lf))
        b_col.append(pltpu.roll(b_all, LANES - nh, 1))
        g_col.append(gates - b_col[s])
        g_row.append(jnp.transpose(g_col[s]))
    scale = hd ** -0.5
    scale_is_pow2 = (hd & (hd - 1)) == 0 and (hd.bit_length() - 1) % 2 == 0
    ones_v = jnp.ones((lc, LANES), BF16)
    reps = hd // LANES
    wide = lambda col: jnp.concatenate([col] * reps, axis=1)

    m_heads, decays, h_heads, c_heads = {}, {}, {}, {}

    def gate_phase(j):
        s, h = divmod(j, nh)
        m_prev = m_scr[j]
        x = jnp.where(causal, g_row[s][h:h + 1, :], -jnp.inf)
        m_run = jnp.maximum(jnp.max(x, axis=1, keepdims=True), m_prev)
        m_run_b = jnp.broadcast_to(m_run, (lc, LANES))
        if lc % LANES == 0:
            d_scr[j] = jnp.exp(x - jnp.concatenate([m_run_b] * (lc // LANES), axis=1))
        else:
            d_scr[j] = jnp.exp(x - m_run)
        m_last = m_run[lc - 1:lc, :]
        m_row = jnp.broadcast_to(b_col[s][:, h:h + 1], (lc, LANES)) + m_run_b
        f_scr[j] = jnp.exp(-m_row)
        fb_scr[j, 0] = jnp.exp(m_prev - m_run_b)
        fb_scr[j, 1] = jnp.exp(jnp.broadcast_to(g_col[s][:, h:h + 1], (lc, LANES)) - m_last)
        decays[j] = jnp.exp(m_prev - m_last)
        m_heads[j] = b_col[s][lc - 1:lc, h:h + 1] + m_last

    def matmul_phase(j):
        s, h = divmod(j, nh)
        sl = slice(h * hd, (h + 1) * hd)
        q = q_ref[s, :, sl]
        if scale_is_pow2:
            k = k_ref[s, :, sl] * scale
        else:
            k = (k_ref[s, :, sl].astype(F32) * scale).astype(BF16)
        v_aug = jnp.concatenate([v_ref[s, :, sl], ones_v], axis=1)
        c_prev = c_scr[j]

        qk = lax.dot_general(q, k, (((1,), (1,)), ((), ())), preferred_element_type=F32)
        sm = (qk * d_scr[j]).astype(BF16)
        qw = (q.astype(F32) * wide(fb_scr[j, 0])).astype(BF16)
        tot = _dot(jnp.concatenate([qw, sm], axis=1),
                   jnp.concatenate([c_prev.astype(BF16), v_aug], axis=0))
        inv = 1.0 / jnp.maximum(jnp.abs(tot[:, hd:]), f_scr[j])
        hg = tot[:, :hd] * wide(inv) * jax.nn.sigmoid(o_ref[s, :, sl].astype(F32))
        h_heads[j] = (_layer_norm(hg) * gnw_ref[:, sl]).astype(BF16)

        kw = (k.astype(F32) * wide(fb_scr[j, 1])).astype(BF16)
        c_heads[j] = decays[j] * c_prev + lax.dot_general(
            kw, v_aug, (((0,), (0,)), ((), ())), preferred_element_type=F32)

    for j in range(nslots):
        gate_phase(j)
    for j in range(nslots):
        matmul_phase(j)

    for s in range(nseq):
        h_ref[s] = jnp.concatenate([h_heads[s * nh + h] for h in range(nh)], axis=1)
    c_scr[...] = jnp.stack([c_heads[j] for j in range(nslots)], axis=0)
    m_scr[...] = jnp.stack([m_heads[j] for j in range(nslots)], axis=0)

    @pl.when(ci == pl.num_programs(1) - 1)
    def _():
        c_out[...] = c_scr[:, :, 0:hd].reshape(nseq, nh, hd, hd)
        n_out[...] = jnp.stack([jnp.transpose(c_scr[j, :, hd:hd + LANES])[0:1, :]
                                for j in range(nslots)], axis=0).reshape(nseq, nh, 1, hd)
        m_out[...] = m_scr[...].reshape(nseq, nh, 1, 1)


def _mlstm(qkvo, ifg, gate_bias, gn_w, state, b, l, lc, nh, nseq):
    mw = gn_w.shape[1]
    hd = mw // nh
    assert hd % LANES == 0 and b % nseq == 0
    nc = l // lc
    nslots = nseq * nh
    seq3 = lambda bi, ci: (bi, ci, 0)
    col = lambda j: (lambda bi, ci: (bi, ci, j))
    st4 = lambda bi, ci: (bi, 0, 0, 0)
    zero_init = state is None
    qkvo3 = qkvo.reshape(b, l, 4 * mw)
    in_specs = [pl.BlockSpec((nseq, lc, mw), col(0)), pl.BlockSpec((nseq, lc, mw), col(1)),
                pl.BlockSpec((nseq, lc, mw), col(2)), pl.BlockSpec((nseq, lc, mw), col(3)),
                pl.BlockSpec((nseq, lc, LANES), seq3),
                pl.BlockSpec((1, LANES), lambda bi, ci: (0, 0)),
                pl.BlockSpec((1, mw), lambda bi, ci: (0, 0))]
    args = [qkvo3, qkvo3, qkvo3, qkvo3, ifg.reshape(b, l, LANES), gate_bias, gn_w]
    if not zero_init:
        c0, n0, m0 = state
        in_specs += [pl.BlockSpec((nseq, nh, hd, hd), st4), pl.BlockSpec((nseq, nh, 1, hd), st4),
                     pl.BlockSpec((nseq, nh, 1, 1), st4)]
        args += [c0, n0.reshape(b, nh, 1, hd), m0.reshape(b, nh, 1, 1)]
    nbytes = (2 * 5 * nseq * lc * mw * 2 + nslots * hd * (hd + LANES) * 4
              + (2 + 2 * (not zero_init)) * nslots * hd * hd * 4
              + nslots * lc * (lc + 3 * LANES) * 4 + 16 * lc * max(lc, hd + LANES) * 4)
    h, c, n, m = pl.pallas_call(
        functools.partial(_mlstm_kernel, nh=nh, hd=hd, zero_init=zero_init),
        grid=(b // nseq, nc),
        in_specs=in_specs,
        out_specs=[pl.BlockSpec((nseq, lc, mw), seq3), pl.BlockSpec((nseq, nh, hd, hd), st4),
                   pl.BlockSpec((nseq, nh, 1, hd), st4), pl.BlockSpec((nseq, nh, 1, 1), st4)],
        out_shape=[jax.ShapeDtypeStruct((b, l, mw), BF16),
                   jax.ShapeDtypeStruct((b, nh, hd, hd), F32),
                   jax.ShapeDtypeStruct((b, nh, 1, hd), F32),
                   jax.ShapeDtypeStruct((b, nh, 1, 1), F32)],
        scratch_shapes=[pltpu.VMEM((nslots, hd, hd + LANES), F32), pltpu.VMEM((nslots, 1, 1), F32),
                        pltpu.VMEM((nslots, lc, lc), F32), pltpu.VMEM((nslots, lc, LANES), F32),
                        pltpu.VMEM((nslots, 2, lc, LANES), F32)],
        compiler_params=_params(nbytes, 2),
        name="mlstm",
    )(*args)
    return h.reshape(b * l, mw), c, n.reshape(b, nh, hd), m.reshape(b, nh)


def _mix_kernel(*refs, start_pos, alpha, has_prefix):
    if has_prefix:
        (p_ref, pre_ref, bo_ref, ga_ref, gb_ref, x_ref, g1_ref, sh2_ref, sc2_ref, wpool_ref, psc_ref,
         wpa_ref, wpb_ref, wout_ref, l1g_ref, l1b_ref, x1_ref, u2_ref, ext_scr) = refs
    else:
        (p_ref, bo_ref, ga_ref, gb_ref, x_ref, g1_ref, sh2_ref, sc2_ref, wpool_ref, psc_ref,
         wpa_ref, wpb_ref, wout_ref, l1g_ref, l1b_ref, x1_ref, u2_ref, ext_scr) = refs
    nb, lt, pw = p_ref.shape
    d = x_ref.shape[2]
    grp = pw // len(POOL_WINDOWS)
    ti = pl.program_id(1)

    @pl.when(ti == 0)
    def _():
        ext_scr[:, 0:POOL_HALO, :] = jnp.zeros((nb, POOL_HALO, pw), F32)
        if has_prefix:
            ext_scr[:, POOL_HALO - POOL_STATE:POOL_HALO, :] = pre_ref[...]

    @pl.when(ti > 0)
    def _():
        ext_scr[:, 0:POOL_HALO, :] = ext_scr[:, lt:lt + POOL_HALO, :]

    ext_scr[:, POOL_HALO:POOL_HALO + lt, :] = p_ref[...]

    pos = start_pos + ti * lt + lax.broadcasted_iota(jnp.int32, (lt, LANES), 0)
    branch = []
    for g, w in enumerate(POOL_WINDOWS):
        cs = slice(g * grp, (g + 1) * grp)
        inv = 1.0 / jnp.minimum(pos + 1, w).astype(F32)
        inv = jnp.concatenate([inv] * (grp // LANES), axis=1)
        ys = []
        for bi in range(nb):
            ext = ext_scr[bi, :, cs]
            acc = ext
            shift = 1
            while shift < w:
                acc = acc + pltpu.roll(acc, shift, 0)
                shift *= 2
            ys.append(acc[POOL_HALO:] * inv - ext[POOL_HALO:])
        y = jnp.concatenate(ys, axis=0).astype(BF16)
        branch.append((_dot(y, wpool_ref[g]) * psc_ref[:, cs]).astype(BF16))
    a_out = jnp.concatenate(branch, axis=1)

    for bs, ts, rs in _row_parts(nb, lt, 2, MIX_PART_ROWS):
        pa = _dot(a_out[rs, :], wpa_ref[...])
        pb = _dot(bo_ref[rs, :], wpb_ref[...])
        merged = (jax.nn.sigmoid(ga_ref[rs, :].astype(F32)) * pa
                  + jax.nn.sigmoid(gb_ref[rs, :].astype(F32)) * pb).astype(BF16)
        t = _dot(merged, wout_ref[...]).reshape(bs.stop - bs.start, ts.stop - ts.start, d)
        x1 = _layer_norm(alpha * x_ref[bs, ts, :] + g1_ref[bs, :, :] * t) * l1g_ref[...] + l1b_ref[...]
        x1_ref[bs, ts, :] = x1
        u2 = _layer_norm(x1) * (1.0 + sc2_ref[bs, :, :]) + sh2_ref[bs, :, :]
        u2_ref[rs, :] = u2.reshape(rs.stop - rs.start, d).astype(BF16)


def _mix(p, prefix, b_out, gg, x, mod4, w_pool, pool_scale, w_pa, w_pb, w_out, ln1_g, ln1_b,
         nb, lt, start_pos, alpha):
    b, l, d = x.shape
    pw = p.shape[1]
    mw = b_out.shape[1]
    assert (pw // len(POOL_WINDOWS)) % LANES == 0
    tm = nb * lt
    nt = l // lt
    rows = lambda bi, ti: (bi * nt + ti, 0)
    seq3 = lambda bi, ti: (bi, ti, 0)
    const2 = lambda bi, ti: (0, 0)
    modk = lambda kk: (lambda bi, ti: (bi, kk, 0, 0))
    has_prefix = prefix is not None
    single = pl.Buffered(1)
    in_specs = [pl.BlockSpec((nb, lt, pw), seq3)]
    args = [p.reshape(b, l, pw)]
    if has_prefix:
        in_specs.append(pl.BlockSpec((nb, POOL_STATE, pw), lambda bi, ti: (bi, 0, 0)))
        args.append(prefix)
    in_specs += [pl.BlockSpec((tm, mw), rows),
                 pl.BlockSpec((tm, d), rows),
                 pl.BlockSpec((tm, d), lambda bi, ti: (bi * nt + ti, 1)),
                 pl.BlockSpec((nb, lt, d), seq3),
                 pl.BlockSpec((nb, None, 1, d), modk(2)),
                 pl.BlockSpec((nb, None, 1, d), modk(3)),
                 pl.BlockSpec((nb, None, 1, d), modk(4)),
                 pl.BlockSpec(w_pool.shape, lambda bi, ti: (0, 0, 0), pipeline_mode=single),
                 pl.BlockSpec((1, pw), const2),
                 pl.BlockSpec(w_pa.shape, const2, pipeline_mode=single),
                 pl.BlockSpec(w_pb.shape, const2, pipeline_mode=single),
                 pl.BlockSpec(w_out.shape, const2, pipeline_mode=single),
                 pl.BlockSpec((1, d), const2),
                 pl.BlockSpec((1, d), const2)]
    args += [b_out, gg, gg, x, mod4, mod4, mod4, w_pool, pool_scale, w_pa, w_pb, w_out, ln1_g, ln1_b]
    wbytes = 2 * (w_pool.size + w_pa.size + w_pb.size + w_out.size)
    nbytes = (wbytes + 2 * tm * (pw * 4 + mw * 2 + 2 * d * 2 + d * 4) + 2 * tm * (d * 4 + d * 2)
              + nb * (POOL_HALO + lt) * pw * 4 + 6 * tm * d * 4)
    x1, u2 = pl.pallas_call(
        functools.partial(_mix_kernel, start_pos=start_pos, alpha=alpha, has_prefix=has_prefix),
        grid=(b // nb, nt),
        in_specs=in_specs,
        out_specs=[pl.BlockSpec((nb, lt, d), seq3), pl.BlockSpec((tm, d), rows)],
        out_shape=[jax.ShapeDtypeStruct((b, l, d), F32), jax.ShapeDtypeStruct((b * l, d), BF16)],
        scratch_shapes=[pltpu.VMEM((nb, POOL_HALO + lt, pw), F32)],
        compiler_params=_params(nbytes, 2),
        name="mix",
    )(*args)
    return x1, u2


def _ffn_up_kernel(u_ref, wg_ref, wu_ref, h_ref, wg_scr, wu_scr):
    @pl.when(pl.program_id(1) == 0)
    def _():
        wg_scr[...] = wg_ref[...].astype(BF16)
        wu_scr[...] = wu_ref[...].astype(BF16)

    half = u_ref.shape[0] // 2
    parts = [slice(0, half), slice(half, 2 * half)] if half % (2 * MIN_MATMUL_ROWS) == 0 else [slice(None)]
    for rows in parts:
        u = u_ref[rows, :]
        hg = _dot(u, wg_scr[...])
        hu = _dot(u, wu_scr[...])
        h_ref[rows, :] = (hg * jax.nn.sigmoid(hg) * hu).astype(BF16)


def _ffn_up(u2, w_gate, w_up, tm, tn):
    m, d = u2.shape
    ff = w_gate.shape[1]
    nbytes = 2 * tm * d * 2 + 2 * (2 * 4 + 2) * d * tn + 2 * tm * tn * 2 + 4 * tm * tn * 4
    return pl.pallas_call(
        _ffn_up_kernel,
        grid=(ff // tn, m // tm),
        in_specs=[pl.BlockSpec((tm, d), lambda j, i: (i, 0)),
                  pl.BlockSpec((d, tn), lambda j, i: (0, j)),
                  pl.BlockSpec((d, tn), lambda j, i: (0, j))],
        out_specs=pl.BlockSpec((tm, tn), lambda j, i: (i, j)),
        out_shape=jax.ShapeDtypeStruct((m, ff), BF16),
        scratch_shapes=[pltpu.VMEM((d, tn), BF16), pltpu.VMEM((d, tn), BF16)],
        compiler_params=_params(nbytes, 2),
        name="ffn_up",
    )(u2, w_gate, w_up)


def _ffn_down_kernel(h_ref, wd_ref, x1_ref, g2_ref, l2g_ref, l2b_ref, y_ref, *, alpha):
    nb, lt, d = x1_ref.shape
    for bs, ts, rs in _row_parts(nb, lt):
        f = _dot(h_ref[rs, :], wd_ref[...]).reshape(bs.stop - bs.start, ts.stop - ts.start, d)
        z = alpha * x1_ref[bs, ts, :] + g2_ref[bs, :, :] * f
        y_ref[bs, ts, :] = _layer_norm(z) * l2g_ref[...] + l2b_ref[...]


def _ffn_down(hidden, x1, mod4, w_down, ln2_g, ln2_b, nb, lt, alpha):
    b, l, d = x1.shape
    ff = w_down.shape[0]
    tm = nb * lt
    nt = l // lt
    const2 = lambda bi, ti: (0, 0)
    seq3 = lambda bi, ti: (bi, ti, 0)
    nbytes = 2 * tm * ff * 2 + ff * d * 2 + 4 * tm * d * 4 + 4 * tm * d * 4
    return pl.pallas_call(
        functools.partial(_ffn_down_kernel, alpha=alpha),
        grid=(b // nb, nt),
        in_specs=[pl.BlockSpec((tm, ff), lambda bi, ti: (bi * nt + ti, 0)),
                  pl.BlockSpec((ff, d), const2, pipeline_mode=pl.Buffered(1)),
                  pl.BlockSpec((nb, lt, d), seq3),
                  pl.BlockSpec((nb, None, 1, d), lambda bi, ti: (bi, 5, 0, 0)),
                  pl.BlockSpec((1, d), const2),
                  pl.BlockSpec((1, d), const2)],
        out_specs=pl.BlockSpec((nb, lt, d), seq3),
        out_shape=jax.ShapeDtypeStruct((b, l, d), F32),
        compiler_params=_params(nbytes, 2),
        name="ffn_down",
    )(hidden, w_down, x1, mod4, ln2_g, ln2_b)


def _tile(total, want):
    if total <= want:
        return total
    t = want
    while total % t or t % 8:
        t -= 1
    return t


def _stream(x, mod, state, start_pos, w, alpha):
    b, l, d = x.shape
    nh = w["nh"]
    pw = w["pool_scale"].shape[1]
    mw = w["gn_w"].shape[1]
    mod4 = mod.reshape(b, 6, 1, d)

    def tiling(rows_want):
        lt = _tile(l, rows_want)
        nb = _tile(b, max(1, rows_want // lt)) if lt == l else 1
        return nb, lt

    nb, lt = tiling(ROWS_INPROJ)
    u, p, ifg = _inproj(x, mod4, w["w_in_t"], pw, w["w_if"], nb, lt)
    tm = _tile(b * l, ROWS_MATMUL)
    qkvo = _matmul(u, w["w_in_t"], pw, 4 * mw, tm, COLS_MATMUL)
    gg = _matmul(u, w["w_in_t"], pw + 4 * mw + 2 * nh, 2 * d, tm, COLS_MATMUL)

    lc = _tile(l, MLSTM_CHUNK)
    mstate = None if state is None else state[1:]
    nseq = max(MLSTM_SEQS, MLSTM_ROWS // lc)
    b_out, c_new, n_new, m_new = _mlstm(qkvo, ifg, w["gate_bias"], w["gn_w"], mstate, b, l, lc, nh,
                                        nseq if b % nseq == 0 else 1)

    nb, lt = tiling(ROWS_MIX)
    prefix = None if state is None else state[0]
    x1, u2 = _mix(p, prefix, b_out, gg, x, mod4, w["w_pool"], w["pool_scale"], w["w_pa"], w["w_pb"],
                  w["w_out"], w["ln1_g"], w["ln1_b"], nb, lt, start_pos, alpha)

    ff = w["w_gate"].shape[1]
    hidden = _ffn_up(u2, w["w_gate"], w["w_up"], tm, COLS_FFN_UP if ff % COLS_FFN_UP == 0 else ff)
    nb, lt = tiling(ROWS_FFN_DOWN)
    y = _ffn_down(hidden, x1, mod4, w["w_down"], w["ln2_g"], w["ln2_b"], nb, lt, alpha)
    pool_state = p.reshape(b, l, pw)[:, l - POOL_STATE:, :]
    return y, pool_state, c_new, n_new, m_new


def kernel(x_prompt, x_sample, c_prompt, c_sample, state_pool, state_mlstm_C, state_mlstm_n, state_mlstm_m, w_ada, b_ada, w_in, b_i, b_f, w_pool, pool_scale, gn_w, w_pa, w_pb, w_out, ln1_g, ln1_b, w_gate, w_up, w_down, ln2_g, ln2_b):
    depth = w_ada.shape[0]
    alpha = (2 * depth) ** 0.25
    d = x_prompt.shape[2]
    nh = b_i.shape[1]
    pw = pool_scale.shape[1]
    mw = gn_w.shape[1]
    nbp = x_prompt.shape[0]
    assert pw % COLS_MATMUL == 0 and mw % COLS_MATMUL == 0 and d % COLS_MATMUL == 0
    y_p, y_s = x_prompt, x_sample
    outs_p, outs_s = [], []
    for li in range(depth):
        wt = jnp.transpose(w_in[li])
        gates0 = pw + 4 * mw
        w = {
            "nh": nh,
            "w_in_t": wt,
            "w_if": jnp.pad(wt[gates0:gates0 + 2 * nh], ((0, LANES - 2 * nh), (0, 0))).astype(BF16),
            "gate_bias": jnp.pad(jnp.concatenate([b_i[li], b_f[li]]), (0, LANES - 2 * nh)).reshape(1, LANES),
            "gn_w": gn_w[li].reshape(1, mw),
            "w_pool": w_pool[li].astype(BF16),
            "pool_scale": pool_scale[li].reshape(1, pw),
            "w_pa": w_pa[li].astype(BF16),
            "w_pb": w_pb[li].astype(BF16),
            "w_out": w_out[li].astype(BF16),
            "ln1_g": ln1_g[li].reshape(1, d),
            "ln1_b": ln1_b[li].reshape(1, d),
            "w_gate": w_gate[li],
            "w_up": w_up[li],
            "w_down": w_down[li].astype(BF16),
            "ln2_g": ln2_g[li].reshape(1, d),
            "ln2_b": ln2_b[li].reshape(1, d),
        }
        mod = _ada(jnp.concatenate([c_prompt, c_sample], axis=0), w_ada[li], b_ada[li])
        res_p = _stream(y_p, mod[:nbp], None, 0, w, alpha)
        res_s = _stream(y_s, mod[nbp:], (state_pool[li], state_mlstm_C[li], state_mlstm_n[li], state_mlstm_m[li]),
                        PAST_LEN, w, alpha)
        y_p, y_s = res_p[0], res_s[0]
        outs_p.append(res_p[1:])
        outs_s.append(res_s[1:])
    stack = lambda outs, k: jnp.stack([o[k] for o in outs], axis=0)
    return (y_p, y_s,
            stack(outs_p, 0), stack(outs_p, 1), stack(outs_p, 2), stack(outs_p, 3),
            stack(outs_s, 0), stack(outs_s, 1), stack(outs_s, 2), stack(outs_s, 3))
```

```python
import functools

import jax
import jax.numpy as jnp
from jax import lax
from jax.experimental import pallas as pl
from jax.experimental.pallas import tpu as pltpu

F32 = jnp.float32
BF16 = jnp.bfloat16

LN_EPS = 1e-5
POOL_WINDOWS = (2, 4, 8, 16)
POOL_STATE = max(POOL_WINDOWS) - 1
POOL_HALO = 16
PAST_LEN = 1024
LANES = 128
VMEM_CAP_BYTES = 60 * 1024 * 1024
VMEM_MIN_BYTES = 32 * 1024 * 1024

ROWS_INPROJ = 1024
ROWS_MATMUL = 2048
COLS_MATMUL = 1024
COLS_FFN_UP = 512
COLS_ADA = 1024
ROWS_MIX = 256
ROWS_FFN_DOWN = 512
MLSTM_CHUNK = 256
MLSTM_SEQS = 2
MLSTM_ROWS = 128


def _vmem_limit(nbytes):
    return int(min(VMEM_CAP_BYTES, max(VMEM_MIN_BYTES, nbytes * 5 // 4)))


def _params(nbytes, ndims):
    return pltpu.CompilerParams(dimension_semantics=("arbitrary",) * ndims,
                                vmem_limit_bytes=_vmem_limit(nbytes))


def _layer_norm(x, eps=LN_EPS):
    mu = jnp.mean(x, axis=-1, keepdims=True)
    xc = x - mu
    var = jnp.mean(xc * xc, axis=-1, keepdims=True)
    return xc * lax.rsqrt(var + eps)


MIN_MATMUL_ROWS = 256


def _row_parts(nb, lt, parts=2):
    while parts > 1:
        if nb == 1 and lt % (parts * MIN_MATMUL_ROWS) == 0:
            h = lt // parts
            return [(slice(0, 1), slice(i * h, (i + 1) * h), slice(i * h, (i + 1) * h)) for i in range(parts)]
        if nb % parts == 0 and (nb // parts * lt) % MIN_MATMUL_ROWS == 0:
            h = nb // parts
            return [(slice(i * h, (i + 1) * h), slice(0, lt), slice(i * h * lt, (i + 1) * h * lt))
                    for i in range(parts)]
        parts //= 2
    return [(slice(0, nb), slice(0, lt), slice(0, nb * lt))]


def _dot(a, b):
    return jnp.dot(a, b, preferred_element_type=F32)


def _dot_nt(a, bt):
    return lax.dot_general(a, bt, (((1,), (1,)), ((), ())), preferred_element_type=F32)


def _log_sigmoid(x):
    return jnp.minimum(x, 0.0) - jnp.log1p(jnp.exp(-jnp.abs(x)))


def _split3(x):
    hi = x.astype(BF16)
    r1 = x - hi.astype(F32)
    mid = r1.astype(BF16)
    lo = (r1 - mid.astype(F32)).astype(BF16)
    return hi, mid, lo


def _ada_kernel(c_ref, w_ref, b_ref, o_ref):
    c = c_ref[...]
    a = (c * jax.nn.sigmoid(c)).astype(BF16)
    o_ref[...] = _dot(a, w_ref[...].astype(BF16)) + b_ref[...]


def _ada(c, w_ada, b_ada):
    nb, d = c.shape
    n = w_ada.shape[1]
    tn = COLS_ADA if n % COLS_ADA == 0 else n
    nbytes = 2 * (d * tn * 4) + d * tn * 2 + 4 * nb * (d + tn) * 4
    return pl.pallas_call(
        _ada_kernel,
        grid=(n // tn,),
        in_specs=[pl.BlockSpec((nb, d), lambda j: (0, 0)),
                  pl.BlockSpec((d, tn), lambda j: (0, j)),
                  pl.BlockSpec((1, tn), lambda j: (0, j))],
        out_specs=pl.BlockSpec((nb, tn), lambda j: (0, j)),
        out_shape=jax.ShapeDtypeStruct((nb, n), F32),
        compiler_params=_params(nbytes, 1),
        name="ada",
    )(c, w_ada, b_ada.reshape(1, n))


def _inproj_kernel(x_ref, sh_ref, sc_ref, wp_ref, wif_ref, u_ref, p_ref, ifg_ref, wp_scr):
    nb, lt, d = x_ref.shape

    @pl.when((pl.program_id(0) == 0) & (pl.program_id(1) == 0))
    def _():
        wp_scr[...] = wp_ref[...].astype(BF16)

    for bs, ts, rs in _row_parts(nb, lt, 4):
        u = _layer_norm(x_ref[bs, ts, :]) * (1.0 + sc_ref[bs, :, :]) + sh_ref[bs, :, :]
        ub = u.reshape(rs.stop - rs.start, d).astype(BF16)
        u_ref[rs, :] = ub
        p_ref[rs, :] = _dot_nt(ub, wp_scr[...])
        ifg_ref[rs, :] = _dot_nt(ub, wif_ref[...])


def _inproj(x, mod4, w_rows, pw, w_if, nb, lt):
    b, l, d = x.shape
    tm = nb * lt
    nt = l // lt
    rows = lambda bi, ti: (bi * nt + ti, 0)
    nbytes = (2 * tm * d * 4 + 2 * tm * d * 2 + 2 * tm * pw * 4 + d * pw * (4 + 2)
              + 3 * tm * d * 4 + tm * pw * 4)
    return pl.pallas_call(
        _inproj_kernel,
        grid=(b // nb, nt),
        in_specs=[pl.BlockSpec((nb, lt, d), lambda bi, ti: (bi, ti, 0)),
                  pl.BlockSpec((nb, None, 1, d), lambda bi, ti: (bi, 0, 0, 0)),
                  pl.BlockSpec((nb, None, 1, d), lambda bi, ti: (bi, 1, 0, 0)),
                  pl.BlockSpec((pw, d), lambda bi, ti: (0, 0), pipeline_mode=pl.Buffered(1)),
                  pl.BlockSpec((LANES, d), lambda bi, ti: (0, 0))],
        out_specs=[pl.BlockSpec((tm, d), rows),
                   pl.BlockSpec((tm, pw), rows),
                   pl.BlockSpec((tm, LANES), rows)],
        out_shape=[jax.ShapeDtypeStruct((b * l, d), BF16),
                   jax.ShapeDtypeStruct((b * l, pw), F32),
                   jax.ShapeDtypeStruct((b * l, LANES), F32)],
        scratch_shapes=[pltpu.VMEM((pw, d), BF16)],
        compiler_params=_params(nbytes, 2),
        name="inproj",
    )(x, mod4, mod4, w_rows, w_if)


def _matmul_kernel(a_ref, b_ref, o_ref, *w_scr):
    if w_scr:
        @pl.when(pl.program_id(1) == 0)
        def _():
            w_scr[0][...] = b_ref[...].astype(BF16)
        w = w_scr[0][...]
    else:
        w = b_ref[...]
    half = a_ref.shape[0] // 2
    if half % (2 * MIN_MATMUL_ROWS) == 0:
        o_ref[:half, :] = _dot_nt(a_ref[:half, :], w).astype(o_ref.dtype)
        o_ref[half:, :] = _dot_nt(a_ref[half:, :], w).astype(o_ref.dtype)
    else:
        o_ref[...] = _dot_nt(a_ref[...], w).astype(o_ref.dtype)


def _matmul(a, b, row0, n, tm, tn):
    m, k = a.shape
    wsize = b.dtype.itemsize
    scratch = [pltpu.VMEM((tn, k), BF16)] if b.dtype == F32 else []
    nbytes = 2 * tm * k * 2 + (2 * wsize + 2) * k * tn + 2 * tm * tn * 2 + tm * tn * 4
    if row0 % tn == 0:
        w_spec = pl.BlockSpec((tn, k), lambda j, i: (j + row0 // tn, 0))
    else:
        assert row0 % 8 == 0
        w_spec = pl.BlockSpec((pl.Element(tn), pl.Element(k)),
                              lambda j, i: (pl.multiple_of(row0 + j * tn, 8), 0))
    return pl.pallas_call(
        _matmul_kernel,
        grid=(n // tn, m // tm),
        in_specs=[pl.BlockSpec((tm, k), lambda j, i: (i, 0)), w_spec],
        out_specs=pl.BlockSpec((tm, tn), lambda j, i: (i, j)),
        out_shape=jax.ShapeDtypeStruct((m, n), BF16),
        scratch_shapes=scratch,
        compiler_params=_params(nbytes, 2),
        name="bigproj",
    )(a, b)


def _mlstm_kernel(*refs, nh, hd, zero_init):
    if zero_init:
        (q_ref, k_ref, v_ref, o_ref, ifg_ref, bias_ref, gnw_ref,
         h_ref, c_out, n_out, m_out, c_scr, m_scr, d_scr, f_scr, fb_scr) = refs
    else:
        (q_ref, k_ref, v_ref, o_ref, ifg_ref, bias_ref, gnw_ref, c0_ref, n0_ref, m0_ref,
         h_ref, c_out, n_out, m_out, c_scr, m_scr, d_scr, f_scr, fb_scr) = refs
    nseq, lc = q_ref.shape[0], q_ref.shape[1]
    nslots = nseq * nh
    ci = pl.program_id(1)

    @pl.when(ci == 0)
    def _():
        if zero_init:
            c_scr[...] = jnp.zeros_like(c_scr)
            m_scr[...] = jnp.zeros_like(m_scr)
        else:
            c_scr[:, :, 0:hd] = c0_ref[...].reshape(nslots, hd, hd)
            for j in range(nslots):
                c_scr[j, :, hd:hd + LANES] = jnp.transpose(
                    jnp.broadcast_to(n0_ref[j // nh, j % nh], (LANES, hd)))
            m_scr[...] = m0_ref[...].reshape(nslots, 1, 1)

    row_t = lax.broadcasted_iota(jnp.int32, (lc, lc), 0)
    col_s = lax.broadcasted_iota(jnp.int32, (lc, lc), 1)
    causal = col_s <= row_t
    tril = jnp.where(causal, 1.0, 0.0).astype(BF16)
    b_col, g_col, g_row = [], [], []
    for s in range(nseq):
        gates = ifg_ref[s] + bias_ref[...]
        lf = _log_sigmoid(gates)
        b_all = sum(_dot(tril, piece) for piece in _split3(lf))
        b_col.append(pltpu.roll(b_all, LANES - nh, 1))
        g_col.append(gates - b_col[s])
        g_row.append(jnp.transpose(g_col[s]))
    scale = hd ** -0.5
    scale_is_pow2 = (hd & (hd - 1)) == 0 and (hd.bit_length() - 1) % 2 == 0
    ones_v = jnp.ones((lc, LANES), BF16)
    reps = hd // LANES
    wide = lambda col: jnp.concatenate([col] * reps, axis=1)

    m_heads, decays, h_heads, c_heads = {}, {}, {}, {}

    def gate_phase(j):
        s, h = divmod(j, nh)
        m_prev = m_scr[j]
        x = jnp.where(causal, g_row[s][h:h + 1, :], -jnp.inf)
        m_run = jnp.maximum(jnp.max(x, axis=1, keepdims=True), m_prev)
        m_run_b = jnp.broadcast_to(m_run, (lc, LANES))
        if lc % LANES == 0:
            d_scr[j] = jnp.exp(x - jnp.concatenate([m_run_b] * (lc // LANES), axis=1))
        else:
            d_scr[j] = jnp.exp(x - m_run)
        m_last = m_run[lc - 1:lc, :]
        m_row = jnp.broadcast_to(b_col[s][:, h:h + 1], (lc, LANES)) + m_run_b
        f_scr[j] = jnp.exp(-m_row)
        fb_scr[j, 0] = jnp.exp(m_prev - m_run_b)
        fb_scr[j, 1] = jnp.exp(jnp.broadcast_to(g_col[s][:, h:h + 1], (lc, LANES)) - m_last)
        decays[j] = jnp.exp(m_prev - m_last)
        m_heads[j] = b_col[s][lc - 1:lc, h:h + 1] + m_last

    def matmul_phase(j):
        s, h = divmod(j, nh)
        sl = slice(h * hd, (h + 1) * hd)
        q = q_ref[s, :, sl]
        if scale_is_pow2:
            k = k_ref[s, :, sl] * scale
        else:
            k = (k_ref[s, :, sl].astype(F32) * scale).astype(BF16)
        v_aug = jnp.concatenate([v_ref[s, :, sl], ones_v], axis=1)
        c_prev = c_scr[j]

        qk = lax.dot_general(q, k, (((1,), (1,)), ((), ())), preferred_element_type=F32)
        sm = (qk * d_scr[j]).astype(BF16)
        qw = (q.astype(F32) * wide(fb_scr[j, 0])).astype(BF16)
        tot = _dot(jnp.concatenate([qw, sm], axis=1),
                   jnp.concatenate([c_prev.astype(BF16), v_aug], axis=0))
        inv = 1.0 / jnp.maximum(jnp.abs(tot[:, hd:]), f_scr[j])
        hg = tot[:, :hd] * wide(inv) * jax.nn.sigmoid(o_ref[s, :, sl].astype(F32))
        h_heads[j] = (_layer_norm(hg) * gnw_ref[:, sl]).astype(BF16)

        kw = (k.astype(F32) * wide(fb_scr[j, 1])).astype(BF16)
        c_heads[j] = decays[j] * c_prev + lax.dot_general(
            kw, v_aug, (((0,), (0,)), ((), ())), preferred_element_type=F32)

    for j in range(nslots):
        gate_phase(j)
    for j in range(nslots):
        matmul_phase(j)

    for s in range(nseq):
        h_ref[s] = jnp.concatenate([h_heads[s * nh + h] for h in range(nh)], axis=1)
    c_scr[...] = jnp.stack([c_heads[j] for j in range(nslots)], axis=0)
    m_scr[...] = jnp.stack([m_heads[j] for j in range(nslots)], axis=0)

    @pl.when(ci == pl.num_programs(1) - 1)
    def _():
        c_out[...] = c_scr[:, :, 0:hd].reshape(nseq, nh, hd, hd)
        n_out[...] = jnp.stack([jnp.transpose(c_scr[j, :, hd:hd + LANES])[0:1, :]
                                for j in range(nslots)], axis=0).reshape(nseq, nh, 1, hd)
        m_out[...] = m_scr[...].reshape(nseq, nh, 1, 1)


def _mlstm(qkvo, ifg, gate_bias, gn_w, state, b, l, lc, nh, nseq):
    mw = gn_w.shape[1]
    hd = mw // nh
    assert hd % LANES == 0 and b % nseq == 0
    nc = l // lc
    nslots = nseq * nh
    seq3 = lambda bi, ci: (bi, ci, 0)
    col = lambda j: (lambda bi, ci: (bi, ci, j))
    st4 = lambda bi, ci: (bi, 0, 0, 0)
    zero_init = state is None
    qkvo3 = qkvo.reshape(b, l, 4 * mw)
    in_specs = [pl.BlockSpec((nseq, lc, mw), col(0)), pl.BlockSpec((nseq, lc, mw), col(1)),
                pl.BlockSpec((nseq, lc, mw), col(2)), pl.BlockSpec((nseq, lc, mw), col(3)),
                pl.BlockSpec((nseq, lc, LANES), seq3),
                pl.BlockSpec((1, LANES), lambda bi, ci: (0, 0)),
                pl.BlockSpec((1, mw), lambda bi, ci: (0, 0))]
    args = [qkvo3, qkvo3, qkvo3, qkvo3, ifg.reshape(b, l, LANES), gate_bias, gn_w]
    if not zero_init:
        c0, n0, m0 = state
        in_specs += [pl.BlockSpec((nseq, nh, hd, hd), st4), pl.BlockSpec((nseq, nh, 1, hd), st4),
                     pl.BlockSpec((nseq, nh, 1, 1), st4)]
        args += [c0, n0.reshape(b, nh, 1, hd), m0.reshape(b, nh, 1, 1)]
    nbytes = (2 * 5 * nseq * lc * mw * 2 + nslots * hd * (hd + LANES) * 4
              + (2 + 2 * (not zero_init)) * nslots * hd * hd * 4
              + nslots * lc * (lc + 3 * LANES) * 4 + 16 * lc * max(lc, hd + LANES) * 4)
    h, c, n, m = pl.pallas_call(
        functools.partial(_mlstm_kernel, nh=nh, hd=hd, zero_init=zero_init),
        grid=(b // nseq, nc),
        in_specs=in_specs,
        out_specs=[pl.BlockSpec((nseq, lc, mw), seq3), pl.BlockSpec((nseq, nh, hd, hd), st4),
                   pl.BlockSpec((nseq, nh, 1, hd), st4), pl.BlockSpec((nseq, nh, 1, 1), st4)],
        out_shape=[jax.ShapeDtypeStruct((b, l, mw), BF16),
                   jax.ShapeDtypeStruct((b, nh, hd, hd), F32),
                   jax.ShapeDtypeStruct((b, nh, 1, hd), F32),
                   jax.ShapeDtypeStruct((b, nh, 1, 1), F32)],
        scratch_shapes=[pltpu.VMEM((nslots, hd, hd + LANES), F32), pltpu.VMEM((nslots, 1, 1), F32),
                        pltpu.VMEM((nslots, lc, lc), F32), pltpu.VMEM((nslots, lc, LANES), F32),
                        pltpu.VMEM((nslots, 2, lc, LANES), F32)],
        compiler_params=_params(nbytes, 2),
        name="mlstm",
    )(*args)
    return h.reshape(b * l, mw), c, n.reshape(b, nh, hd), m.reshape(b, nh)


def _mix_kernel(*refs, start_pos, alpha, has_prefix):
    if has_prefix:
        (p_ref, pre_ref, bo_ref, ga_ref, gb_ref, x_ref, g1_ref, sh2_ref, sc2_ref, wpool_ref, psc_ref,
         wpa_ref, wpb_ref, wout_ref, l1g_ref, l1b_ref, x1_ref, u2_ref, ext_scr) = refs
    else:
        (p_ref, bo_ref, ga_ref, gb_ref, x_ref, g1_ref, sh2_ref, sc2_ref, wpool_ref, psc_ref,
         wpa_ref, wpb_ref, wout_ref, l1g_ref, l1b_ref, x1_ref, u2_ref, ext_scr) = refs
    nb, lt, pw = p_ref.shape
    d = x_ref.shape[2]
    grp = pw // len(POOL_WINDOWS)
    ti = pl.program_id(1)

    @pl.when(ti == 0)
    def _():
        ext_scr[:, 0:POOL_HALO, :] = jnp.zeros((nb, POOL_HALO, pw), F32)
        if has_prefix:
            ext_scr[:, POOL_HALO - POOL_STATE:POOL_HALO, :] = pre_ref[...]

    @pl.when(ti > 0)
    def _():
        ext_scr[:, 0:POOL_HALO, :] = ext_scr[:, lt:lt + POOL_HALO, :]

    ext_scr[:, POOL_HALO:POOL_HALO + lt, :] = p_ref[...]

    pos = start_pos + ti * lt + lax.broadcasted_iota(jnp.int32, (lt, LANES), 0)
    branch = []
    for g, w in enumerate(POOL_WINDOWS):
        cs = slice(g * grp, (g + 1) * grp)
        inv = 1.0 / jnp.minimum(pos + 1, w).astype(F32)
        inv = jnp.concatenate([inv] * (grp // LANES), axis=1)
        ys = []
        for bi in range(nb):
            ext = ext_scr[bi, :, cs]
            acc = ext
            shift = 1
            while shift < w:
                acc = acc + pltpu.roll(acc, shift, 0)
                shift *= 2
            ys.append(acc[POOL_HALO:] * inv - ext[POOL_HALO:])
        y = jnp.concatenate(ys, axis=0).astype(BF16)
        branch.append((_dot(y, wpool_ref[g]) * psc_ref[:, cs]).astype(BF16))
    a_out = jnp.concatenate(branch, axis=1)

    pa = _dot(a_out, wpa_ref[...])
    pb = _dot(bo_ref[...], wpb_ref[...])
    merged = (jax.nn.sigmoid(ga_ref[...].astype(F32)) * pa
              + jax.nn.sigmoid(gb_ref[...].astype(F32)) * pb).astype(BF16)
    t = _dot(merged, wout_ref[...]).reshape(nb, lt, d)
    x1 = (_layer_norm(alpha * x_ref[...] + g1_ref[...] * t) * (alpha * l1g_ref[...])
          + alpha * l1b_ref[...])
    x1_ref[...] = x1
    u2 = _layer_norm(x1, LN_EPS * alpha * alpha) * (1.0 + sc2_ref[...]) + sh2_ref[...]
    u2_ref[...] = u2.reshape(nb * lt, d).astype(BF16)


def _mix(p, prefix, b_out, gg, x, mod4, w_pool, pool_scale, w_pa, w_pb, w_out, ln1_g, ln1_b,
         nb, lt, start_pos, alpha):
    b, l, d = x.shape
    pw = p.shape[1]
    mw = b_out.shape[1]
    assert (pw // len(POOL_WINDOWS)) % LANES == 0
    tm = nb * lt
    nt = l // lt
    rows = lambda bi, ti: (bi * nt + ti, 0)
    seq3 = lambda bi, ti: (bi, ti, 0)
    const2 = lambda bi, ti: (0, 0)
    modk = lambda kk: (lambda bi, ti: (bi, kk, 0, 0))
    has_prefix = prefix is not None
    single = pl.Buffered(1)
    in_specs = [pl.BlockSpec((nb, lt, pw), seq3)]
    args = [p.reshape(b, l, pw)]
    if has_prefix:
        in_specs.append(pl.BlockSpec((nb, POOL_STATE, pw), lambda bi, ti: (bi, 0, 0)))
        args.append(prefix)
    in_specs += [pl.BlockSpec((tm, mw), rows),
                 pl.BlockSpec((tm, d), rows),
                 pl.BlockSpec((tm, d), lambda bi, ti: (bi * nt + ti, 1)),
                 pl.BlockSpec((nb, lt, d), seq3),
                 pl.BlockSpec((nb, None, 1, d), modk(2)),
                 pl.BlockSpec((nb, None, 1, d), modk(3)),
                 pl.BlockSpec((nb, None, 1, d), modk(4)),
                 pl.BlockSpec(w_pool.shape, lambda bi, ti: (0, 0, 0), pipeline_mode=single),
                 pl.BlockSpec((1, pw), const2),
                 pl.BlockSpec(w_pa.shape, const2, pipeline_mode=single),
                 pl.BlockSpec(w_pb.shape, const2, pipeline_mode=single),
                 pl.BlockSpec(w_out.shape, const2, pipeline_mode=single),
                 pl.BlockSpec((1, d), const2),
                 pl.BlockSpec((1, d), const2)]
    args += [b_out, gg, gg, x, mod4, mod4, mod4, w_pool, pool_scale, w_pa, w_pb, w_out, ln1_g, ln1_b]
    wbytes = 2 * (w_pool.size + w_pa.size + w_pb.size + w_out.size)
    nbytes = (wbytes + 2 * tm * (pw * 4 + mw * 2 + 2 * d * 2 + d * 4) + 2 * tm * (d * 4 + d * 2)
              + nb * (POOL_HALO + lt) * pw * 4 + 6 * tm * d * 4)
    x1, u2 = pl.pallas_call(
        functools.partial(_mix_kernel, start_pos=start_pos, alpha=alpha, has_prefix=has_prefix),
        grid=(b // nb, nt),
        in_specs=in_specs,
        out_specs=[pl.BlockSpec((nb, lt, d), seq3), pl.BlockSpec((tm, d), rows)],
        out_shape=[jax.ShapeDtypeStruct((b, l, d), F32), jax.ShapeDtypeStruct((b * l, d), BF16)],
        scratch_shapes=[pltpu.VMEM((nb, POOL_HALO + lt, pw), F32)],
        compiler_params=_params(nbytes, 2),
        name="mix",
    )(*args)
    return x1, u2


def _ffn_up_kernel(u_ref, wg_ref, wu_ref, h_ref, wg_scr, wu_scr):
    @pl.when(pl.program_id(1) == 0)
    def _():
        wg_scr[...] = wg_ref[...].astype(BF16)
        wu_scr[...] = wu_ref[...].astype(BF16)

    half = u_ref.shape[0] // 2
    parts = [slice(0, half), slice(half, 2 * half)] if half % (2 * MIN_MATMUL_ROWS) == 0 else [slice(None)]
    for rows in parts:
        u = u_ref[rows, :]
        hg = _dot(u, wg_scr[...])
        hu = _dot(u, wu_scr[...])
        h_ref[rows, :] = (hg * jax.nn.sigmoid(hg) * hu).astype(BF16)


def _ffn_up(u2, w_gate, w_up, tm, tn):
    m, d = u2.shape
    ff = w_gate.shape[1]
    nbytes = 2 * tm * d * 2 + 2 * (2 * 4 + 2) * d * tn + 2 * tm * tn * 2 + 4 * tm * tn * 4
    return pl.pallas_call(
        _ffn_up_kernel,
        grid=(ff // tn, m // tm),
        in_specs=[pl.BlockSpec((tm, d), lambda j, i: (i, 0)),
                  pl.BlockSpec((d, tn), lambda j, i: (0, j)),
                  pl.BlockSpec((d, tn), lambda j, i: (0, j))],
        out_specs=pl.BlockSpec((tm, tn), lambda j, i: (i, j)),
        out_shape=jax.ShapeDtypeStruct((m, ff), BF16),
        scratch_shapes=[pltpu.VMEM((d, tn), BF16), pltpu.VMEM((d, tn), BF16)],
        compiler_params=_params(nbytes, 2),
        name="ffn_up",
    )(u2, w_gate, w_up)


def _ffn_down_kernel(h_ref, wd_ref, x1_ref, g2_ref, l2g_ref, l2b_ref, y_ref, *, alpha):
    nb, lt, d = x1_ref.shape
    for bs, ts, rs in _row_parts(nb, lt):
        f = _dot(h_ref[rs, :], wd_ref[...]).reshape(bs.stop - bs.start, ts.stop - ts.start, d)
        z = x1_ref[bs, ts, :] + g2_ref[bs, :, :] * f
        y_ref[bs, ts, :] = _layer_norm(z) * l2g_ref[...] + l2b_ref[...]


def _ffn_down(hidden, x1, mod4, w_down, ln2_g, ln2_b, nb, lt, alpha):
    b, l, d = x1.shape
    ff = w_down.shape[0]
    tm = nb * lt
    nt = l // lt
    const2 = lambda bi, ti: (0, 0)
    seq3 = lambda bi, ti: (bi, ti, 0)
    nbytes = 2 * tm * ff * 2 + ff * d * 2 + 4 * tm * d * 4 + 4 * tm * d * 4
    return pl.pallas_call(
        functools.partial(_ffn_down_kernel, alpha=alpha),
        grid=(b // nb, nt),
        in_specs=[pl.BlockSpec((tm, ff), lambda bi, ti: (bi * nt + ti, 0)),
                  pl.BlockSpec((ff, d), const2, pipeline_mode=pl.Buffered(1)),
                  pl.BlockSpec((nb, lt, d), seq3),
                  pl.BlockSpec((nb, None, 1, d), lambda bi, ti: (bi, 5, 0, 0)),
                  pl.BlockSpec((1, d), const2),
                  pl.BlockSpec((1, d), const2)],
        out_specs=pl.BlockSpec((nb, lt, d), seq3),
        out_shape=jax.ShapeDtypeStruct((b, l, d), F32),
        compiler_params=_params(nbytes, 2),
        name="ffn_down",
    )(hidden, w_down, x1, mod4, ln2_g, ln2_b)


def _tile(total, want):
    if total <= want:
        return total
    t = want
    while total % t or t % 8:
        t -= 1
    return t


def _stream(x, mod, state, start_pos, w, alpha):
    b, l, d = x.shape
    nh = w["nh"]
    pw = w["pool_scale"].shape[1]
    mw = w["gn_w"].shape[1]
    mod4 = mod.reshape(b, 6, 1, d)

    def tiling(rows_want):
        lt = _tile(l, rows_want)
        nb = _tile(b, max(1, rows_want // lt)) if lt == l else 1
        return nb, lt

    nb, lt = tiling(ROWS_INPROJ)
    u, p, ifg = _inproj(x, mod4, w["w_in_t"], pw, w["w_if"], nb, lt)
    tm = _tile(b * l, ROWS_MATMUL)
    qkvo = _matmul(u, w["w_in_t"], pw, 4 * mw, tm, COLS_MATMUL)
    gg = _matmul(u, w["w_in_t"], pw + 4 * mw + 2 * nh, 2 * d, tm, COLS_MATMUL)

    lc = _tile(l, MLSTM_CHUNK)
    mstate = None if state is None else state[1:]
    nseq = max(MLSTM_SEQS, MLSTM_ROWS // lc)
    b_out, c_new, n_new, m_new = _mlstm(qkvo, ifg, w["gate_bias"], w["gn_w"], mstate, b, l, lc, nh,
                                        nseq if b % nseq == 0 else 1)

    nb, lt = tiling(ROWS_MIX)
    prefix = None if state is None else state[0]
    x1, u2 = _mix(p, prefix, b_out, gg, x, mod4, w["w_pool"], w["pool_scale"], w["w_pa"], w["w_pb"],
                  w["w_out"], w["ln1_g"], w["ln1_b"], nb, lt, start_pos, alpha)

    ff = w["w_gate"].shape[1]
    hidden = _ffn_up(u2, w["w_gate"], w["w_up"], tm, COLS_FFN_UP if ff % COLS_FFN_UP == 0 else ff)
    nb, lt = tiling(ROWS_FFN_DOWN)
    y = _ffn_down(hidden, x1, mod4, w["w_down"], w["ln2_g"], w["ln2_b"], nb, lt, alpha)
    pool_state = p.reshape(b, l, pw)[:, l - POOL_STATE:, :]
    return y, pool_state, c_new, n_new, m_new


def kernel(x_prompt, x_sample, c_prompt, c_sample, state_pool, state_mlstm_C, state_mlstm_n, state_mlstm_m, w_ada, b_ada, w_in, b_i, b_f, w_pool, pool_scale, gn_w, w_pa, w_pb, w_out, ln1_g, ln1_b, w_gate, w_up, w_down, ln2_g, ln2_b):
    depth = w_ada.shape[0]
    alpha = (2 * depth) ** 0.25
    d = x_prompt.shape[2]
    nh = b_i.shape[1]
    pw = pool_scale.shape[1]
    mw = gn_w.shape[1]
    nbp = x_prompt.shape[0]
    assert pw % COLS_MATMUL == 0 and mw % COLS_MATMUL == 0 and d % COLS_MATMUL == 0
    y_p, y_s = x_prompt, x_sample
    outs_p, outs_s = [], []
    for li in range(depth):
        wt = jnp.transpose(w_in[li])
        gates0 = pw + 4 * mw
        w = {
            "nh": nh,
            "w_in_t": wt,
            "w_if": jnp.pad(wt[gates0:gates0 + 2 * nh], ((0, LANES - 2 * nh), (0, 0))).astype(BF16),
            "gate_bias": jnp.pad(jnp.concatenate([b_i[li], b_f[li]]), (0, LANES - 2 * nh)).reshape(1, LANES),
            "gn_w": gn_w[li].reshape(1, mw),
            "w_pool": w_pool[li].astype(BF16),
            "pool_scale": pool_scale[li].reshape(1, pw),
            "w_pa": w_pa[li].astype(BF16),
            "w_pb": w_pb[li].astype(BF16),
            "w_out": w_out[li].astype(BF16),
            "ln1_g": ln1_g[li].reshape(1, d),
            "ln1_b": ln1_b[li].reshape(1, d),
            "w_gate": w_gate[li],
            "w_up": w_up[li],
            "w_down": w_down[li].astype(BF16),
            "ln2_g": ln2_g[li].reshape(1, d),
            "ln2_b": ln2_b[li].reshape(1, d),
        }
        mod = _ada(jnp.concatenate([c_prompt, c_sample], axis=0), w_ada[li], b_ada[li])
        res_p = _stream(y_p, mod[:nbp], None, 0, w, alpha)
        res_s = _stream(y_s, mod[nbp:], (state_pool[li], state_mlstm_C[li], state_mlstm_n[li], state_mlstm_m[li]),
                        PAST_LEN, w, alpha)
        y_p, y_s = res_p[0], res_s[0]
        outs_p.append(res_p[1:])
        outs_s.append(res_s[1:])
    stack = lambda outs, k: jnp.stack([o[k] for o in outs], axis=0)
    return (y_p, y_s,
            stack(outs_p, 0), stack(outs_p, 1), stack(outs_p, 2), stack(outs_p, 3),
            stack(outs_s, 0), stack(outs_s, 1), stack(outs_s, 2), stack(outs_s, 3))
```
